```python
import math
import jax, jax.numpy as jnp
from jax import lax
import numpy as np

D_MODEL = 1024
BATCH = 8
SEQ = 4096
DEPTH = 2

HEAD_DIM = 64
N_HEADS_DIFF = 4
N_HEADS_FOX = 4
N_HEADS_SB = 4
N_HEADS_DSA = 4
N_IDX_HEADS = 8
IDX_DIM = 64
TOPK_MAX = 256
ROPE_THETA = 500000.0
ROPE_DIM = HEAD_DIM // 4
Q_BLOCK = 128
D_FF = 2816
N_BRANCH = 4
NORM_EPS = 1e-6
SUBLN_EPS = 1e-5

W_DIFF = N_HEADS_DIFF * 2 * HEAD_DIM
W_FOX = N_HEADS_FOX * HEAD_DIM
W_SB = N_HEADS_SB * HEAD_DIM
W_DSA = N_HEADS_DSA * HEAD_DIM
IN_SPLITS = (
    W_DIFF, W_DIFF, W_DIFF,
    W_FOX, W_FOX, W_FOX, N_HEADS_FOX,
    W_SB, W_SB, W_SB,
    W_DSA, W_DSA, W_DSA, N_IDX_HEADS * IDX_DIM, IDX_DIM, N_IDX_HEADS,
)
IN_WIDTH = sum(IN_SPLITS)

kernel_name = "hybrid_gated_four_mixer_decoder"


def rms_norm(x, g, eps=NORM_EPS):
    xf = x.astype(jnp.float32)
    y = xf * lax.rsqrt(jnp.mean(xf * xf, axis=-1, keepdims=True) + eps)
    return (y * g.astype(jnp.float32)).astype(x.dtype)


def swiglu(x, w_gu, w_down):
    g, u = jnp.split(x @ w_gu, 2, axis=-1)
    return (jax.nn.silu(g) * u) @ w_down


def rope_tables(positions):
    freqs = ROPE_THETA ** (-jnp.arange(0, ROPE_DIM, 2, dtype=jnp.float32) / ROPE_DIM)
    ang = positions.astype(jnp.float32)[:, None] * freqs[None, :]
    return jnp.cos(ang), jnp.sin(ang)


def partial_rope(x, cos, sin):
    shape = (1, cos.shape[0]) + (1,) * (x.ndim - 3) + (cos.shape[1],)
    c = cos.reshape(shape).astype(x.dtype)
    s = sin.reshape(shape).astype(x.dtype)
    half = ROPE_DIM // 2
    x1, x2, xp = x[..., :half], x[..., half:ROPE_DIM], x[..., ROPE_DIM:]
    return jnp.concatenate([x1 * c - x2 * s, x2 * c + x1 * s, xp], axis=-1)


def sweep_query_blocks(block_fn, seq_len):
    n_blocks = seq_len // Q_BLOCK
    out = lax.map(block_fn, jnp.arange(n_blocks, dtype=jnp.int32) * Q_BLOCK)
    nb, b, blk, h, e = out.shape
    return jnp.moveaxis(out, 0, 1).reshape(b, nb * blk, h, e)


def diff_attention(q, k, v, lam):
    S = q.shape[1]
    scale = HEAD_DIM ** -0.5
    kpos = jnp.arange(S)

    def block(start):
        qb = lax.dynamic_slice_in_dim(q, start, Q_BLOCK, axis=1)
        qpos = start + jnp.arange(Q_BLOCK)
        causal = kpos[None, :] <= qpos[:, None]
        logits = jnp.einsum('bqhcd,bkhcd->bchqk', qb, k).astype(jnp.float32) * scale
        p = jax.nn.softmax(jnp.where(causal, logits, -jnp.inf), axis=-1)
        p = p[:, 0] - lam * p[:, 1]
        return jnp.einsum('bhqk,bkhe->bqhe', p.astype(v.dtype), v)

    return sweep_query_blocks(block, S)


def forgetting_attention(q, k, v, f_logits):
    S = q.shape[1]
    scale = HEAD_DIM ** -0.5
    kpos = jnp.arange(S)
    cum = jnp.transpose(jnp.cumsum(jax.nn.log_sigmoid(f_logits.astype(jnp.float32)), axis=1), (0, 2, 1))

    def block(start):
        qb = lax.dynamic_slice_in_dim(q, start, Q_BLOCK, axis=1)
        cq = lax.dynamic_slice_in_dim(cum, start, Q_BLOCK, axis=2)
        qpos = start + jnp.arange(Q_BLOCK)
        causal = kpos[None, :] <= qpos[:, None]
        logits = (jnp.einsum('bqhd,bkhd->bhqk', qb, k).astype(jnp.float32) * scale
                  + cq[..., :, None] - cum[:, :, None, :])
        p = jax.nn.softmax(jnp.where(causal, logits, -jnp.inf), axis=-1)
        return jnp.einsum('bhqk,bkhd->bqhd', p.astype(v.dtype), v)

    return sweep_query_blocks(block, S)


def stick_breaking_attention(q, k, v):
    S = q.shape[1]
    scale = HEAD_DIM ** -0.5
    kpos = jnp.arange(S)

    def block(start):
        qb = lax.dynamic_slice_in_dim(q, start, Q_BLOCK, axis=1)
        qpos = start + jnp.arange(Q_BLOCK)
        strict = kpos[None, :] < qpos[:, None]
        z = jnp.einsum('bqhd,bkhd->bhqk', qb, k).astype(jnp.float32) * scale
        log_1m = jnp.where(strict, jax.nn.log_sigmoid(-z), 0.0)
        after = lax.cumsum(log_1m, axis=3, reverse=True) - log_1m
        a = jnp.where(strict, jnp.exp(jax.nn.log_sigmoid(z) + after), 0.0)
        return jnp.einsum('bhqk,bkhd->bqhd', a.astype(v.dtype), v)

    return sweep_query_blocks(block, S)


def indexed_sparse_attention(q, k, v, iq, ik, iw, topk):
    B, S = q.shape[0], q.shape[1]
    scale = HEAD_DIM ** -0.5
    kpos = jnp.arange(S)
    bidx = jnp.arange(B)[:, None, None]

    def block(start):
        qb = lax.dynamic_slice_in_dim(q, start, Q_BLOCK, axis=1)
        iqb = lax.dynamic_slice_in_dim(iq, start, Q_BLOCK, axis=1)
        iwb = lax.dynamic_slice_in_dim(iw, start, Q_BLOCK, axis=1)
        qpos = start + jnp.arange(Q_BLOCK)
        causal = kpos[None, :] <= qpos[:, None]
        rel = jax.nn.relu(jnp.einsum('bqhd,bkd->bqhk', iqb, ik).astype(jnp.float32))
        score = jnp.einsum('bqhk,bqh->bqk', rel, iwb.astype(jnp.float32))
        score = jnp.where(causal[None], score, -jnp.inf)
        _, idx = lax.top_k(score, topk)
        valid = idx <= qpos[None, :, None]
        k_sel = k[bidx, idx]
        v_sel = v[bidx, idx]
        logits = jnp.einsum('bqhd,bqjhd->bhqj', qb, k_sel).astype(jnp.float32) * scale
        p = jax.nn.softmax(jnp.where(valid[:, None], logits, -jnp.inf), axis=-1)
        return jnp.einsum('bhqj,bqjhd->bqhd', p.astype(v.dtype), v_sel)

    return sweep_query_blocks(block, S)


def hybrid_mixer(h, cos, sin, layer_idx, w_in, b_fgt, lam_q1, lam_k1, lam_q2, lam_k2, diff_gain,
                 w_gate, b_gate, w_br_a, w_br_b, w_br_c, w_br_d, w_out):
    B, S, D = h.shape
    points = np.cumsum(IN_SPLITS)[:-1].tolist()
    (aq, ak, av, bq, bk, bv, bf, cq, ck, cv,
     dq, dk, dv, diq, dik, diw) = jnp.split(h @ w_in, points, axis=-1)

    lam_init = 0.8 - 0.6 * math.exp(-0.3 * layer_idx)
    lam = (jnp.exp(jnp.sum(lam_q1.astype(jnp.float32) * lam_k1.astype(jnp.float32)))
           - jnp.exp(jnp.sum(lam_q2.astype(jnp.float32) * lam_k2.astype(jnp.float32))) + lam_init)
    qa = partial_rope(aq.reshape(B, S, N_HEADS_DIFF, 2, HEAD_DIM), cos, sin)
    ka = partial_rope(ak.reshape(B, S, N_HEADS_DIFF, 2, HEAD_DIM), cos, sin)
    ya = diff_attention(qa, ka, av.reshape(B, S, N_HEADS_DIFF, 2 * HEAD_DIM), lam)
    ya = (rms_norm(ya, diff_gain, SUBLN_EPS) * (1.0 - lam_init)).reshape(B, S, W_DIFF)

    yb = forgetting_attention(bq.reshape(B, S, N_HEADS_FOX, HEAD_DIM),
                              bk.reshape(B, S, N_HEADS_FOX, HEAD_DIM),
                              bv.reshape(B, S, N_HEADS_FOX, HEAD_DIM),
                              bf + b_fgt).reshape(B, S, W_FOX)

    yc = stick_breaking_attention(cq.reshape(B, S, N_HEADS_SB, HEAD_DIM),
                                  ck.reshape(B, S, N_HEADS_SB, HEAD_DIM),
                                  cv.reshape(B, S, N_HEADS_SB, HEAD_DIM)).reshape(B, S, W_SB)

    topk = min(TOPK_MAX, S // 4)
    qd = partial_rope(dq.reshape(B, S, N_HEADS_DSA, HEAD_DIM), cos, sin)
    kd = partial_rope(dk.reshape(B, S, N_HEADS_DSA, HEAD_DIM), cos, sin)
    iq = partial_rope(diq.reshape(B, S, N_IDX_HEADS, IDX_DIM), cos, sin)
    ik = partial_rope(dik, cos, sin)
    yd = indexed_sparse_attention(qd, kd, dv.reshape(B, S, N_HEADS_DSA, HEAD_DIM),
                                  iq, ik, diw, topk).reshape(B, S, W_DSA)

    g = jax.nn.sigmoid(h @ w_gate + b_gate).reshape(B, S, N_BRANCH, D)
    merged = (g[:, :, 0] * (ya @ w_br_a) + g[:, :, 1] * (yb @ w_br_b)
              + g[:, :, 2] * (yc @ w_br_c) + g[:, :, 3] * (yd @ w_br_d))
    return merged @ w_out


def setup_inputs(seed: int = 0) -> dict:
    key = jax.random.key(seed)
    ks = iter(jax.random.split(key, 32))
    f32 = jnp.float32

    def nrm(shape, scale):
        return jax.random.normal(next(ks), shape, f32) * scale

    def gain(shape):
        return 1.0 + 0.02 * jax.random.normal(next(ks), shape, f32)

    L, D, F = DEPTH, D_MODEL, D_FF
    return {
        "x": jax.random.normal(next(ks), (BATCH, SEQ, D), f32),
        "positions": jnp.arange(SEQ, dtype=jnp.int32),
        "ffn1_norm": gain((L, D)),
        "ffn1_w_gu": nrm((L, D, 2 * F), D ** -0.5),
        "ffn1_w_down": nrm((L, F, D), F ** -0.5),
        "mix_norm": gain((L, D)),
        "w_in": nrm((L, D, IN_WIDTH), D ** -0.5),
        "b_fgt": 2.0 + 0.5 * jax.random.normal(next(ks), (L, N_HEADS_FOX), f32),
        "lam_q1": nrm((L, HEAD_DIM), 0.1),
        "lam_k1": nrm((L, HEAD_DIM), 0.1),
        "lam_q2": nrm((L, HEAD_DIM), 0.1),
        "lam_k2": nrm((L, HEAD_DIM), 0.1),
        "diff_gain": gain((L, 2 * HEAD_DIM)),
        "w_gate": nrm((L, D, N_BRANCH * D), D ** -0.5),
        "b_gate": nrm((L, N_BRANCH * D), 0.01),
        "w_br_a": nrm((L, W_DIFF, D), W_DIFF ** -0.5),
        "w_br_b": nrm((L, W_FOX, D), W_FOX ** -0.5),
        "w_br_c": nrm((L, W_SB, D), W_SB ** -0.5),
        "w_br_d": nrm((L, W_DSA, D), W_DSA ** -0.5),
        "w_out": nrm((L, D, D), 0.5 * D ** -0.5),
        "ffn2_norm": gain((L, D)),
        "ffn2_w_gu": nrm((L, D, 2 * F), D ** -0.5),
        "ffn2_w_down": nrm((L, F, D), F ** -0.5),
        "final_norm": gain((D,)),
    }


def reference(x, positions, ffn1_norm, ffn1_w_gu, ffn1_w_down, mix_norm, w_in, b_fgt,
              lam_q1, lam_k1, lam_q2, lam_k2, diff_gain, w_gate, b_gate,
              w_br_a, w_br_b, w_br_c, w_br_d, w_out, ffn2_norm, ffn2_w_gu, ffn2_w_down,
              final_norm):
    cos, sin = rope_tables(positions)
    for l in range(DEPTH):
        x = x + 0.5 * swiglu(rms_norm(x, ffn1_norm[l]), ffn1_w_gu[l], ffn1_w_down[l])
        x = x + hybrid_mixer(rms_norm(x, mix_norm[l]), cos, sin, l, w_in[l], b_fgt[l],
                             lam_q1[l], lam_k1[l], lam_q2[l], lam_k2[l], diff_gain[l],
                             w_gate[l], b_gate[l], w_br_a[l], w_br_b[l], w_br_c[l], w_br_d[l],
                             w_out[l])
        x = x + 0.5 * swiglu(rms_norm(x, ffn2_norm[l]), ffn2_w_gu[l], ffn2_w_down[l])
    return rms_norm(x, final_norm)
```

```python
import functools
import math

import numpy as np
import jax
import jax.numpy as jnp
from jax import lax
from jax.experimental import pallas as pl
from jax.experimental.pallas import tpu as pltpu

D_MODEL = 1024
HEAD_DIM = 64
N_IDX_HEADS = 8
TOPK_MAX = 256
ROPE_THETA = 500000.0
ROPE_DIM = HEAD_DIM // 4
D_FF = 2816
N_BRANCH = 4
NORM_EPS = 1e-6
SUBLN_EPS = 1e-5
SCALE = HEAD_DIM ** -0.5

LANES = 128
VMEM_LIMIT = 56 * 1024 * 1024

BLK_AQ, BLK_AK, BLK_AV = 0, 4, 8
BLK_BQ, BLK_BK, BLK_BV = 12, 14, 16
BLK_CQ, BLK_CK, BLK_CV = 18, 20, 22
BLK_DQ, BLK_DK, BLK_DV = 24, 26, 28
BLK_IK, BLK_SPARE, BLK_IQ = 30, 31, 32
N_PROJ_BLKS = 36
PROJ_W = N_PROJ_BLKS * LANES
ROPE_BLKS = frozenset(list(range(0, 8)) + [24, 25, 26, 27, 30] + list(range(32, 36)))
SM_IW, SM_F = 0, 8

NEG_BIG = -1e30
INT_MIN = -(2 ** 31)


def _cparams(sem, vmem=VMEM_LIMIT):
    return pltpu.CompilerParams(dimension_semantics=sem, vmem_limit_bytes=vmem)


def _resident(shape):
    nd = len(shape)
    return pl.BlockSpec(shape, lambda *_: (0,) * nd, pipeline_mode=pl.Buffered(1))


def _rms(x, g, eps):
    ms = jnp.mean(x * x, axis=-1, keepdims=True)
    return x * lax.rsqrt(ms + eps) * g


def _dot(a, b):
    return jnp.dot(a, b, preferred_element_type=jnp.float32)


def _dot_nt(a, b):
    return lax.dot_general(a, b, (((1,), (1,)), ((), ())), preferred_element_type=jnp.float32)


def _ffn_kernel(x_ref, g_ref, wg_ref, wu_ref, wd_ref, fg_ref, o_ref, *, tf, final_norm):
    x = x_ref[...]
    hb = _rms(x, g_ref[...], NORM_EPS).astype(jnp.bfloat16)
    acc = jnp.zeros(x.shape, jnp.float32)
    for j in range(D_FF // tf):
        sl = slice(j * tf, (j + 1) * tf)
        g = _dot(hb, wg_ref[:, sl])
        u = _dot(hb, wu_ref[:, sl])
        a = (g * jax.nn.sigmoid(g) * u).astype(jnp.bfloat16)
        acc = acc + _dot(a, wd_ref[sl, :])
    y = x + 0.5 * acc
    if final_norm:
        y = _rms(y, fg_ref[...], NORM_EPS)
    o_ref[...] = y


def _ffn(x2, gain, wg, wu, wd, fgain, *, final_norm, tm=512, tf=256):
    m, d = x2.shape
    return pl.pallas_call(
        functools.partial(_ffn_kernel, tf=tf, final_norm=final_norm),
        grid=(m // tm,),
        in_specs=[
            pl.BlockSpec((tm, d), lambda i: (i, 0)),
            _resident((1, d)),
            _resident((d, D_FF)),
            _resident((d, D_FF)),
            _resident((D_FF, d)),
            _resident((1, d)),
        ],
        out_specs=pl.BlockSpec((tm, d), lambda i: (i, 0)),
        out_shape=jax.ShapeDtypeStruct((m, d), jnp.float32),
        compiler_params=_cparams(("parallel",)),
        name="ffn",
    )(x2, gain, wg, wu, wd, fgain)


def _rope_kernel(pos_ref, freq_ref, m1_ref, m2_ref, c_ref, s1_ref, s2_ref):
    ang = pos_ref[...].astype(jnp.float32) * freq_ref[...]
    c = jnp.cos(ang)
    s = jnp.sin(ang)
    c_ref[...] = c
    s1_ref[...] = s * m1_ref[...]
    s2_ref[...] = s * m2_ref[...]


def _rope_tables(positions):
    s = positions.shape[0]
    freqs = ROPE_THETA ** (-jnp.arange(0, ROPE_DIM, 2, dtype=jnp.float32) / ROPE_DIM)
    half = ROPE_DIM // 2
    d = np.arange(LANES) % HEAD_DIM
    freq_row = jnp.where(d < ROPE_DIM, freqs[d % half], 0.0).reshape(1, LANES)
    m1 = jnp.asarray(np.where(d < half, -1.0, 0.0).astype(np.float32)).reshape(1, LANES)
    m2 = jnp.asarray(np.where((d >= half) & (d < ROPE_DIM), 1.0, 0.0).astype(np.float32)).reshape(1, LANES)
    tab = jax.ShapeDtypeStruct((s, LANES), jnp.float32)
    ts = min(s, 1024)
    row = pl.BlockSpec((1, LANES), lambda i: (0, 0))
    blk = pl.BlockSpec((ts, LANES), lambda i: (i, 0))
    return pl.pallas_call(
        _rope_kernel,
        grid=(s // ts,),
        in_specs=[pl.BlockSpec((ts, 1), lambda i: (i, 0)), row, row, row],
        out_specs=[blk, blk, blk],
        out_shape=[tab, tab, tab],
        compiler_params=_cparams(("parallel",)),
        name="rope_tables",
    )(positions.reshape(s, 1), freq_row, m1, m2)


def _proj_kernel(x_ref, g_ref, w_ref, wsm_ref, bsm_ref, c_ref, s1_ref, s2_ref, p_ref, sm_ref):
    hb = _rms(x_ref[...], g_ref[...], NORM_EPS).astype(jnp.bfloat16)
    c, s1, s2 = c_ref[...], s1_ref[...], s2_ref[...]
    chunk = 4
    for cb in range(N_PROJ_BLKS // chunk):
        y = _dot(hb, w_ref[:, cb * chunk * LANES:(cb + 1) * chunk * LANES])
        for sub in range(chunk):
            blk = cb * chunk + sub
            ys = y[:, sub * LANES:(sub + 1) * LANES]
            if blk in ROPE_BLKS:
                ys = (ys * c + pltpu.roll(ys, LANES - ROPE_DIM // 2, 1) * s1
                      + pltpu.roll(ys, ROPE_DIM // 2, 1) * s2)
            p_ref[:, blk * LANES:(blk + 1) * LANES] = ys.astype(jnp.bfloat16)
    sm_ref[...] = _dot(hb, wsm_ref[...]) + bsm_ref[...]


def _proj(x2, gain, w, wsm, bsm, c, s1, s2, seq, *, tm=512):
    m, d = x2.shape
    tm = min(tm, seq)
    nrep = seq // tm
    tab = pl.BlockSpec((tm, LANES), lambda i: (i % nrep, 0))
    return pl.pallas_call(
        _proj_kernel,
        grid=(m // tm,),
        in_specs=[
            pl.BlockSpec((tm, d), lambda i: (i, 0)),
            _resident((1, d)),
            _resident((d, PROJ_W)),
            _resident((d, LANES)),
            _resident((1, LANES)),
            tab, tab, tab,
        ],
        out_specs=[pl.BlockSpec((tm, PROJ_W), lambda i: (i, 0)),
                   pl.BlockSpec((tm, LANES), lambda i: (i, 0))],
        out_shape=[jax.ShapeDtypeStruct((m, PROJ_W), jnp.bfloat16),
                   jax.ShapeDtypeStruct((m, LANES), jnp.float32)],
        compiler_params=_cparams(("parallel",)),
        name="proj",
    )(x2, gain, w, wsm, bsm, c, s1, s2)


def _cumsum_kernel(sm_ref, cn_ref, ct_ref):
    t = sm_ref[0].T
    n = t.shape[1]
    ls = jnp.minimum(t, 0.0) - jnp.log(1.0 + jnp.exp(-jnp.abs(t)))
    lane = lax.broadcasted_iota(jnp.int32, ls.shape, 1)
    sh = 1
    while sh < n:
        ls = ls + jnp.where(lane >= sh, pltpu.roll(ls, sh, 1), 0.0)
        sh *= 2
    ct_ref[0] = ls[SM_F:SM_F + 8, :]
    cn_ref[0] = ls.T


def _cumsum(small3):
    b, s, _ = small3.shape
    return pl.pallas_call(
        _cumsum_kernel,
        grid=(b,),
        in_specs=[pl.BlockSpec((1, s, LANES), lambda i: (i, 0, 0))],
        out_specs=[pl.BlockSpec((1, s, LANES), lambda i: (i, 0, 0)),
                   pl.BlockSpec((1, 8, s), lambda i: (i, 0, 0))],
        out_shape=[jax.ShapeDtypeStruct((b, s, LANES), jnp.float32),
                   jax.ShapeDtypeStruct((b, 8, s), jnp.float32)],
        compiler_params=_cparams(("parallel",)),
        name="forget_cumsum",
    )(small3)


def _half_masks(dtype):
    lane = lax.broadcasted_iota(jnp.int32, (1, LANES), 1)
    lo = (lane < HEAD_DIM).astype(dtype)
    return lo, (1 - lo).astype(dtype)


def _causal_mask(tq, tk, strict):
    r = lax.broadcasted_iota(jnp.int32, (tq, tk), 0)
    c = lax.broadcasted_iota(jnp.int32, (tq, tk), 1)
    return (c < r) if strict else (c <= r)


def _softmax_step(s, m, l):
    m_new = jnp.maximum(m, jnp.max(s, axis=-1, keepdims=True))
    alpha = jnp.exp(m - m_new)
    p = jnp.exp(s - m_new)
    l_new = alpha * l + jnp.sum(p, axis=-1, keepdims=True)
    return m_new, l_new, alpha, p


def _diff_kernel(q_ref, k_ref, v_ref, lam_ref, gain_ref, o_ref, acc1_ref, acc2_ref, *, t, lam_init):
    qi = pl.program_id(2)
    lo, hi = _half_masks(jnp.float32)
    q = q_ref[0].astype(jnp.float32) * SCALE
    q1 = (q * lo).astype(jnp.bfloat16)
    q2 = (q * hi).astype(jnp.bfloat16)
    acc1_ref[...] = jnp.zeros_like(acc1_ref)
    acc2_ref[...] = jnp.zeros_like(acc2_ref)

    def step(j, carry, masked):
        m1, l1, m2, l2 = carry
        off = pl.multiple_of(j * t, t)
        k = k_ref[0, pl.ds(off, t), :]
        v = v_ref[0, pl.ds(off, t), :]
        s1 = _dot_nt(q1, k)
        s2 = _dot_nt(q2, k)
        if masked:
            cm = _causal_mask(t, t, False)
            s1 = jnp.where(cm, s1, -jnp.inf)
            s2 = jnp.where(cm, s2, -jnp.inf)
        m1, l1, a1, p1 = _softmax_step(s1, m1, l1)
        m2, l2, a2, p2 = _softmax_step(s2, m2, l2)
        acc1_ref[...] = a1 * acc1_ref[...] + _dot(p1.astype(jnp.bfloat16), v)
        acc2_ref[...] = a2 * acc2_ref[...] + _dot(p2.astype(jnp.bfloat16), v)
        return m1, l1, m2, l2

    init_m = jnp.full((t, 1), NEG_BIG, jnp.float32)
    init_l = jnp.zeros((t, 1), jnp.float32)
    carry = lax.fori_loop(0, qi, functools.partial(step, masked=False),
                          (init_m, init_l, init_m, init_l))
    _, l1, _, l2 = step(qi, carry, True)

    lp = lam_ref[...]
    lam = (jnp.exp(jnp.sum(lp[0:1] * lp[1:2], axis=-1, keepdims=True))
           - jnp.exp(jnp.sum(lp[2:3] * lp[3:4], axis=-1, keepdims=True)) + lam_init)
    y = acc1_ref[...] / l1 - lam * (acc2_ref[...] / l2)
    y = _rms(y, gain_ref[...], SUBLN_EPS) * (1.0 - lam_init)
    o_ref[0] = y.astype(o_ref.dtype)


def _diff_attention(p3, lam_rows, gain, layer_idx, *, t=256):
    b, s, _ = p3.shape
    t = min(t, s)
    lam_init = 0.8 - 0.6 * math.exp(-0.3 * layer_idx)
    nh = 4
    return pl.pallas_call(
        functools.partial(_diff_kernel, t=t, lam_init=lam_init),
        grid=(b, nh, s // t),
        in_specs=[
            pl.BlockSpec((1, t, LANES), lambda bi, h, qi: (bi, qi, BLK_AQ + h)),
            pl.BlockSpec((1, s, LANES), lambda bi, h, qi: (bi, 0, BLK_AK + h)),
            pl.BlockSpec((1, s, LANES), lambda bi, h, qi: (bi, 0, BLK_AV + h)),
            pl.BlockSpec((8, LANES), lambda bi, h, qi: (0, 0)),
            pl.BlockSpec((1, LANES), lambda bi, h, qi: (0, 0)),
        ],
        out_specs=pl.BlockSpec((1, t, LANES), lambda bi, h, qi: (bi, qi, h)),
        out_shape=jax.ShapeDtypeStruct((b, s, nh * LANES), jnp.bfloat16),
        scratch_shapes=[pltpu.VMEM((t, LANES), jnp.float32), pltpu.VMEM((t, LANES), jnp.float32)],
        compiler_params=_cparams(("parallel", "parallel", "arbitrary")),
        name="diff_attn",
    )(p3, p3, p3, lam_rows, gain)


def _fox_kernel(q_ref, k_ref, v_ref, cn_ref, ct_ref, o_ref, acc_ref, *, t):
    pair = pl.program_id(1)
    qi = pl.program_id(2)
    lo, hi = _half_masks(jnp.float32)
    lob, hib = _half_masks(jnp.bfloat16)
    q = q_ref[0].astype(jnp.float32) * SCALE
    qs = ((q * lo).astype(jnp.bfloat16), (q * hi).astype(jnp.bfloat16))
    vms = (lob, hib)
    lane = lax.broadcasted_iota(jnp.int32, (1, LANES), 1)
    cn = cn_ref[0]
    cqs = tuple(jnp.sum(jnp.where(lane == SM_F + 2 * pair + i, cn, 0.0), axis=-1, keepdims=True)
                for i in range(2))
    acc_ref[...] = jnp.zeros_like(acc_ref)

    def step(j, carry, masked):
        off = pl.multiple_of(j * t, t)
        k = k_ref[0, pl.ds(off, t), :]
        v = v_ref[0, pl.ds(off, t), :]
        new = []
        pv = None
        alphas = []
        for i in range(2):
            m, l = carry[2 * i], carry[2 * i + 1]
            ck = ct_ref[0, pl.ds(2 * pair + i, 1), pl.ds(off, t)]
            s = _dot_nt(qs[i], k) + (cqs[i] - ck)
            if masked:
                s = jnp.where(_causal_mask(t, t, False), s, -jnp.inf)
            m, l, a, p = _softmax_step(s, m, l)
            new += [m, l]
            alphas.append(a)
            d = _dot(p.astype(jnp.bfloat16), v * vms[i])
            pv = d if pv is None else pv + d
        alpha = jnp.where(lane < HEAD_DIM, alphas[0], alphas[1])
        acc_ref[...] = alpha * acc_ref[...] + pv
        return tuple(new)

    init_m = jnp.full((t, 1), NEG_BIG, jnp.float32)
    init_l = jnp.zeros((t, 1), jnp.float32)
    carry = lax.fori_loop(0, qi, functools.partial(step, masked=False),
                          (init_m, init_l, init_m, init_l))
    _, l0, _, l1 = step(qi, carry, True)
    o_ref[0] = (acc_ref[...] / jnp.where(lane < HEAD_DIM, l0, l1)).astype(o_ref.dtype)


def _fox_attention(p3, cum_n, cum_t, *, t=256):
    b, s, _ = p3.shape
    t = min(t, s)
    return pl.pallas_call(
        functools.partial(_fox_kernel, t=t),
        grid=(b, 2, s // t),
        in_specs=[
            pl.BlockSpec((1, t, LANES), lambda bi, p, qi: (bi, qi, BLK_BQ + p)),
            pl.BlockSpec((1, s, LANES), lambda bi, p, qi: (bi, 0, BLK_BK + p)),
            pl.BlockSpec((1, s, LANES), lambda bi, p, qi: (bi, 0, BLK_BV + p)),
            pl.BlockSpec((1, t, LANES), lambda bi, p, qi: (bi, qi, 0)),
            pl.BlockSpec((1, 8, s), lambda bi, p, qi: (bi, 0, 0)),
        ],
        out_specs=pl.BlockSpec((1, t, LANES), lambda bi, p, qi: (bi, qi, p)),
        out_shape=jax.ShapeDtypeStruct((b, s, 2 * LANES), jnp.bfloat16),
        scratch_shapes=[pltpu.VMEM((t, LANES), jnp.float32)],
        compiler_params=_cparams(("parallel", "parallel", "arbitrary")),
        name="fox_attn",
    )(p3, p3, p3, cum_n, cum_t)


def _sb_kernel(q_ref, k_ref, v_ref, o_ref, acc_ref, *, t):
    qi = pl.program_id(2)
    lo, hi = _half_masks(jnp.float32)
    lob, hib = _half_masks(jnp.bfloat16)
    q = q_ref[0].astype(jnp.float32) * SCALE
    qs = ((q * lo).astype(jnp.bfloat16), (q * hi).astype(jnp.bfloat16))
    vms = (lob, hib)
    r = lax.broadcasted_iota(jnp.int32, (t, t), 0)
    c = lax.broadcasted_iota(jnp.int32, (t, t), 1)
    upper = (r > c).astype(jnp.bfloat16)
    acc_ref[...] = jnp.zeros_like(acc_ref)

    def step(j, carry, masked):
        off = pl.multiple_of(j * t, t)
        k = k_ref[0, pl.ds(off, t), :]
        v = v_ref[0, pl.ds(off, t), :]
        new = []
        pv = None
        for i in range(2):
            run = carry[i]
            z = _dot_nt(qs[i], k)
            sp = jnp.maximum(z, 0.0) + jnp.log(1.0 + jnp.exp(-jnp.abs(z)))
            l1m = -sp
            if masked:
                cm = _causal_mask(t, t, True)
                l1m = jnp.where(cm, l1m, 0.0)
            l_hi = l1m.astype(jnp.bfloat16)
            l_lo = (l1m - l_hi.astype(jnp.float32)).astype(jnp.bfloat16)
            after = _dot(l_hi, upper) + _dot(l_lo, upper) + run
            a = jnp.exp(z - sp + after)
            if masked:
                a = jnp.where(cm, a, 0.0)
            d = _dot(a.astype(jnp.bfloat16), v * vms[i])
            pv = d if pv is None else pv + d
            new.append(run + jnp.sum(l1m, axis=-1, keepdims=True))
        acc_ref[...] += pv
        return tuple(new)

    zero = jnp.zeros((t, 1), jnp.float32)
    carry = step(qi, (zero, zero), True)
    lax.fori_loop(0, qi, lambda i, cr: step(qi - 1 - i, cr, False), carry)
    o_ref[0] = acc_ref[...].astype(o_ref.dtype)


def _sb_attention(p3, *, t=256):
    b, s, _ = p3.shape
    t = min(t, s)
    return pl.pallas_call(
        functools.partial(_sb_kernel, t=t),
        grid=(b, 2, s // t),
        in_specs=[
            pl.BlockSpec((1, t, LANES), lambda bi, p, qi: (bi, qi, BLK_CQ + p)),
            pl.BlockSpec((1, s, LANES), lambda bi, p, qi: (bi, 0, BLK_CK + p)),
            pl.BlockSpec((1, s, LANES), lambda bi, p, qi: (bi, 0, BLK_CV + p)),
        ],
        out_specs=pl.BlockSpec((1, t, LANES), lambda bi, p, qi: (bi, qi, p)),
        out_shape=jax.ShapeDtypeStruct((b, s, 2 * LANES), jnp.bfloat16),
        scratch_shapes=[pltpu.VMEM((t, LANES), jnp.float32)],
        compiler_params=_cparams(("parallel", "parallel", "arbitrary")),
        name="sb_attn",
    )(p3, p3, p3)


def _dsa_kernel(iq_ref, q_ref, sm_ref, ik_ref, k_ref, v_ref, o_ref,
                keys_ref, bias_ref, vt_ref, sig_ref, *, tq, ck, topk, seq):
    qi = pl.program_id(1)
    nblk = (qi * tq + tq + ck - 1) // ck
    lo, hi = _half_masks(jnp.float32)
    halves = (lo, hi)

    @pl.when(qi == 0)
    def _():
        for c in range(seq // ck):
            vt_ref[:, c * ck:(c + 1) * ck] = v_ref[0, c * ck:(c + 1) * ck, :].astype(jnp.float32).T.astype(jnp.bfloat16)

    iw_t = sm_ref[0].T[SM_IW:SM_IW + N_IDX_HEADS, :]
    iq = iq_ref[0].astype(jnp.float32)
    iqs = [(iq[:, (h // 2) * LANES:(h // 2 + 1) * LANES] * halves[h % 2]).astype(jnp.bfloat16)
           for h in range(N_IDX_HEADS)]
    qd = q_ref[0].astype(jnp.float32) * SCALE
    qds = [(qd[:, (h // 2) * LANES:(h // 2 + 1) * LANES] * halves[h % 2]).astype(jnp.bfloat16)
           for h in range(4)]
    t_idx = qi * tq + lax.broadcasted_iota(jnp.int32, (1, tq), 1)
    row = lax.broadcasted_iota(jnp.int32, (ck, tq), 0)

    def score_chunk(c, _):
        off = pl.multiple_of(c * ck, ck)
        ikc = ik_ref[0, pl.ds(off, ck), :]
        acc = jnp.zeros((ck, tq), jnp.float32)
        for h in range(N_IDX_HEADS):
            acc = acc + iw_t[h:h + 1, :] * jnp.maximum(_dot_nt(ikc, iqs[h]), 0.0)
        bits = pltpu.bitcast(acc, jnp.int32)
        key = bits ^ ((bits >> 31) & 0x7FFFFFFF)
        keys_ref[pl.ds(off, ck), :] = jnp.where(row + off <= t_idx, key, INT_MIN)
        return 0

    lax.fori_loop(0, nblk, score_chunk, 0)

    def count(pred_fn):
        def body(c, cnt):
            off = pl.multiple_of(c * ck, ck)
            hit = pred_fn(keys_ref[pl.ds(off, ck), :], off).astype(jnp.int32)
            return cnt + jnp.sum(hit.reshape(ck // 8, 8, tq), axis=0)
        cnt8 = lax.fori_loop(0, nblk, body, jnp.zeros((8, tq), jnp.int32))
        return jnp.sum(cnt8, axis=0, keepdims=True)

    c0 = count(lambda kk, off: kk >= 0)
    tau0 = jnp.where(c0 >= topk, 0, INT_MIN).astype(jnp.int32)

    def bit_step(i, tau):
        cand = tau + jnp.left_shift(jnp.int32(1), 30 - i)
        cnt = count(lambda kk, off: kk >= cand)
        return jnp.where(cnt >= topk, cand, tau)

    tau = lax.fori_loop(0, 31, bit_step, tau0)
    cnt_ge = count(lambda kk, off: kk >= tau)
    real = tau != INT_MIN
    sig_ref[...] = jnp.where(real, seq, -1).astype(jnp.int32)
    over = jnp.max(jnp.where(real & (cnt_ge > topk), 1, 0))

    @pl.when(over > 0)
    def _():
        cnt_gt = count(lambda kk, off: kk > tau)
        need = topk - cnt_gt

        def idx_step(i, x):
            cand = x + jnp.left_shift(jnp.int32(1), (seq.bit_length() - 2) - i)
            f = count(lambda kk, off: (kk == tau) & (row + off < cand))
            return jnp.where(f < need, cand, x)

        x = lax.fori_loop(0, seq.bit_length() - 1, idx_step, jnp.zeros((1, tq), jnp.int32))
        sig_ref[...] = jnp.where(real, x, -1)

    sigma = sig_ref[...]

    def bias_chunk(c, _):
        off = pl.multiple_of(c * ck, ck)
        kk = keys_ref[pl.ds(off, ck), :]
        sel = (kk > tau) | ((kk == tau) & (row + off <= sigma))
        bias_ref[pl.ds(off, ck), :] = jnp.where(sel, 0.0, -jnp.inf)
        return 0

    lax.fori_loop(0, nblk, bias_chunk, 0)

    def attn_chunk(c, carry):
        off = pl.multiple_of(c * ck, ck)
        bias = bias_ref[pl.ds(off, ck), :]
        new = []
        for h in range(4):
            m, l, acc = carry[3 * h], carry[3 * h + 1], carry[3 * h + 2]
            kc = k_ref[0, pl.ds(off, ck), (h // 2) * LANES:(h // 2 + 1) * LANES]
            s = _dot_nt(kc, qds[h]) + bias
            m_new = jnp.maximum(m, jnp.max(s, axis=0, keepdims=True))
            alpha = jnp.exp(m - m_new)
            p = jnp.exp(s - m_new)
            l = alpha * l + jnp.sum(p, axis=0, keepdims=True)
            vt = vt_ref[h * HEAD_DIM:(h + 1) * HEAD_DIM, pl.ds(off, ck)]
            acc = alpha * acc + _dot(vt, p.astype(jnp.bfloat16))
            new += [m_new, l, acc]
        return tuple(new)

    init = []
    for h in range(4):
        init += [jnp.full((1, tq), NEG_BIG, jnp.float32), jnp.zeros((1, tq), jnp.float32),
                 jnp.zeros((HEAD_DIM, tq), jnp.float32)]
    fin = lax.fori_loop(0, nblk, attn_chunk, tuple(init))
    out_t = jnp.concatenate([fin[3 * h + 2] / fin[3 * h + 1] for h in range(4)], axis=0)
    o_ref[0] = out_t.T.astype(o_ref.dtype)


def _dsa_attention(p3, small3, *, tq=128, ck=512):
    b, s, _ = p3.shape
    ck = min(ck, s)
    topk = min(TOPK_MAX, s // 4)
    return pl.pallas_call(
        functools.partial(_dsa_kernel, tq=tq, ck=ck, topk=topk, seq=s),
        grid=(b, s // tq),
        in_specs=[
            pl.BlockSpec((1, tq, 4 * LANES), lambda bi, qi: (bi, qi, BLK_IQ // 4)),
            pl.BlockSpec((1, tq, 2 * LANES), lambda bi, qi: (bi, qi, BLK_DQ // 2)),
            pl.BlockSpec((1, tq, LANES), lambda bi, qi: (bi, qi, 0)),
            pl.BlockSpec((1, s, LANES), lambda bi, qi: (bi, 0, BLK_IK)),
            pl.BlockSpec((1, s, 2 * LANES), lambda bi, qi: (bi, 0, BLK_DK // 2)),
            pl.BlockSpec((1, s, 2 * LANES), lambda bi, qi: (bi, 0, BLK_DV // 2)),
        ],
        out_specs=pl.BlockSpec((1, tq, 2 * LANES), lambda bi, qi: (bi, qi, 0)),
        out_shape=jax.ShapeDtypeStruct((b, s, 2 * LANES), jnp.bfloat16),
        scratch_shapes=[
            pltpu.VMEM((s, tq), jnp.int32),
            pltpu.VMEM((s, tq), jnp.float32),
            pltpu.VMEM((4 * HEAD_DIM, s), jnp.bfloat16),
            pltpu.VMEM((1, tq), jnp.int32),
        ],
        compiler_params=_cparams(("arbitrary", "arbitrary")),
        name="dsa_attn",
    )(p3, p3, small3, p3, p3, p3)


def _merge_kernel(x_ref, g_ref, wgate_ref, bgate_ref, ya_ref, yb_ref, yc_ref, yd_ref,
                  wa_ref, wb_ref, wc_ref, wd_ref, wo_ref, o_ref):
    x = x_ref[...]
    d = x.shape[1]
    hb = _rms(x, g_ref[...], NORM_EPS).astype(jnp.bfloat16)
    merged = jnp.zeros(x.shape, jnp.float32)
    for i, (y_ref, w_ref) in enumerate(((ya_ref, wa_ref), (yb_ref, wb_ref), (yc_ref, wc_ref), (yd_ref, wd_ref))):
        gate = jax.nn.sigmoid(_dot(hb, wgate_ref[:, i * d:(i + 1) * d]) + bgate_ref[:, i * d:(i + 1) * d])
        merged = merged + gate * _dot(y_ref[...], w_ref[...])
    o_ref[...] = x + _dot(merged.astype(jnp.bfloat16), wo_ref[...])


def _merge(x2, gain, wgate, bgate, ya, yb, yc, yd, wa, wb, wc, wd, wo, *, tm=512):
    m, d = x2.shape
    rowblk = lambda w: pl.BlockSpec((tm, w), lambda i: (i, 0))
    return pl.pallas_call(
        _merge_kernel,
        grid=(m // tm,),
        in_specs=[
            rowblk(d), _resident((1, d)), _resident((d, N_BRANCH * d)), _resident((1, N_BRANCH * d)),
            rowblk(ya.shape[1]), rowblk(yb.shape[1]), rowblk(yc.shape[1]), rowblk(yd.shape[1]),
            _resident(wa.shape), _resident(wb.shape), _resident(wc.shape), _resident(wd.shape),
            _resident(wo.shape),
        ],
        out_specs=rowblk(d),
        out_shape=jax.ShapeDtypeStruct((m, d), jnp.float32),
        compiler_params=_cparams(("parallel",)),
        name="merge",
    )(x2, gain, wgate, bgate, ya, yb, yc, yd, wa, wb, wc, wd, wo)


def _proj_weights(w_in, b_fgt):
    d = w_in.shape[0]
    o = 0

    def take(width):
        nonlocal o
        seg = w_in[:, o:o + width]
        o += width
        return seg

    aq, ak, av = take(512), take(512), take(512)
    bq, bk, bv, bf = take(256), take(256), take(256), take(4)
    cq, ck, cv = take(256), take(256), take(256)
    dq, dk, dv = take(256), take(256), take(256)
    diq, dik, diw = take(512), take(64), take(8)
    zeros = jnp.zeros((d, LANES), w_in.dtype)
    big = jnp.concatenate([aq, ak, av, bq, bk, bv, cq, ck, cv, dq, dk, dv, dik, dik, zeros, diq], axis=1)
    small = jnp.concatenate([diw, bf, jnp.zeros((d, LANES - 12), w_in.dtype)], axis=1)
    bias = jnp.concatenate([jnp.zeros((8,), jnp.float32), b_fgt.astype(jnp.float32),
                            jnp.zeros((LANES - 12,), jnp.float32)]).reshape(1, LANES)
    return big.astype(jnp.bfloat16), small.astype(jnp.bfloat16), bias


def kernel(x, positions, ffn1_norm, ffn1_w_gu, ffn1_w_down, mix_norm, w_in, b_fgt, lam_q1, lam_k1, lam_q2, lam_k2, diff_gain, w_gate, b_gate, w_br_a, w_br_b, w_br_c, w_br_d, w_out, ffn2_norm, ffn2_w_gu, ffn2_w_down, final_norm):
    b, s, d = x.shape
    depth = w_in.shape[0]
    bf = jnp.bfloat16
    cos_t, sin1_t, sin2_t = _rope_tables(positions)
    fgain = final_norm.reshape(1, d)
    x2 = x.reshape(b * s, d)
    for l in range(depth):
        x2 = _ffn(x2, ffn1_norm[l].reshape(1, d), ffn1_w_gu[l][:, :D_FF].astype(bf),
                  ffn1_w_gu[l][:, D_FF:].astype(bf), ffn1_w_down[l].astype(bf), fgain, final_norm=False)
        wbig, wsmall, bsmall = _proj_weights(w_in[l], b_fgt[l])
        p2, small2 = _proj(x2, mix_norm[l].reshape(1, d), wbig, wsmall, bsmall, cos_t, sin1_t, sin2_t, s)
        p3 = p2.reshape(b, s, PROJ_W)
        small3 = small2.reshape(b, s, LANES)
        cum_n, cum_t = _cumsum(small3)
        lam_rows = jnp.zeros((8, LANES), jnp.float32)
        lam_rows = lam_rows.at[0:4, 0:HEAD_DIM].set(jnp.stack([lam_q1[l], lam_k1[l], lam_q2[l], lam_k2[l]]))
        ya = _diff_attention(p3, lam_rows, diff_gain[l].reshape(1, LANES), l)
        yb = _fox_attention(p3, cum_n, cum_t)
        yc = _sb_attention(p3)
        yd = _dsa_attention(p3, small3)
        x2 = _merge(x2, mix_norm[l].reshape(1, d), w_gate[l].astype(bf), b_gate[l].reshape(1, N_BRANCH * d),
                    ya.reshape(b * s, -1), yb.reshape(b * s, -1), yc.reshape(b * s, -1), yd.reshape(b * s, -1),
                    w_br_a[l].astype(bf), w_br_b[l].astype(bf), w_br_c[l].astype(bf), w_br_d[l].astype(bf),
                    w_out[l].astype(bf))
        x2 = _ffn(x2, ffn2_norm[l].reshape(1, d), ffn2_w_gu[l][:, :D_FF].astype(bf),
                  ffn2_w_gu[l][:, D_FF:].astype(bf), ffn2_w_down[l].astype(bf), fgain,
                  final_norm=(l == depth - 1))
    return x2.reshape(b, s, d)
```

```python
import functools
import math

import numpy as np
import jax
import jax.numpy as jnp
from jax import lax
from jax.experimental import pallas as pl
from jax.experimental.pallas import tpu as pltpu

D_MODEL = 1024
HEAD_DIM = 64
N_IDX_HEADS = 8
TOPK_MAX = 256
ROPE_THETA = 500000.0
ROPE_DIM = HEAD_DIM // 4
D_FF = 2816
N_BRANCH = 4
NORM_EPS = 1e-6
SUBLN_EPS = 1e-5
SCALE = HEAD_DIM ** -0.5
LOG2E = math.log2(math.e)

LANES = 128
VMEM_LIMIT = 56 * 1024 * 1024

BLK_AQ, BLK_AK, BLK_AV = 0, 4, 8
BLK_BQ, BLK_BK, BLK_BV = 12, 14, 16
BLK_CQ, BLK_CK, BLK_CV = 18, 20, 22
BLK_DQ, BLK_DK, BLK_DV = 24, 26, 28
BLK_IK, BLK_SPARE, BLK_IQ = 30, 31, 32
N_PROJ_BLKS = 36
PROJ_W = N_PROJ_BLKS * LANES
ROPE_BLKS = frozenset(list(range(0, 8)) + [24, 25, 26, 27, 30] + list(range(32, 36)))
SM_IW, SM_F = 0, 8

NEG_BIG = -1e30
INT_MIN = -(2 ** 31)


def _cparams(sem, vmem=VMEM_LIMIT):
    return pltpu.CompilerParams(dimension_semantics=sem, vmem_limit_bytes=vmem)


def _resident(shape):
    nd = len(shape)
    return pl.BlockSpec(shape, lambda *_: (0,) * nd, pipeline_mode=pl.Buffered(1))


def _rms(x, g, eps):
    ms = jnp.mean(x * x, axis=-1, keepdims=True)
    return x * lax.rsqrt(ms + eps) * g


def _dot(a, b):
    return jnp.dot(a, b, preferred_element_type=jnp.float32)


def _dot_nt(a, b):
    return lax.dot_general(a, b, (((1,), (1,)), ((), ())), preferred_element_type=jnp.float32)


def _ffn_kernel(x_ref, g_ref, wg_ref, wu_ref, wd_ref, fg_ref, o_ref, *, tf, final_norm):
    x = x_ref[...]
    hb = _rms(x, g_ref[...], NORM_EPS).astype(jnp.bfloat16)
    acc = jnp.zeros(x.shape, jnp.float32)
    for j in range(D_FF // tf):
        sl = slice(j * tf, (j + 1) * tf)
        g = _dot(hb, wg_ref[:, sl])
        u = _dot(hb, wu_ref[:, sl])
        a = (g * jax.nn.sigmoid(g) * u).astype(jnp.bfloat16)
        acc = acc + _dot(a, wd_ref[sl, :])
    y = x + 0.5 * acc
    if final_norm:
        y = _rms(y, fg_ref[...], NORM_EPS)
    o_ref[...] = y


def _ffn(x2, gain, wg, wu, wd, fgain, *, final_norm, tm=512, tf=256):
    m, d = x2.shape
    return pl.pallas_call(
        functools.partial(_ffn_kernel, tf=tf, final_norm=final_norm),
        grid=(m // tm,),
        in_specs=[
            pl.BlockSpec((tm, d), lambda i: (i, 0)),
            _resident((1, d)),
            _resident((d, D_FF)),
            _resident((d, D_FF)),
            _resident((D_FF, d)),
            _resident((1, d)),
        ],
        out_specs=pl.BlockSpec((tm, d), lambda i: (i, 0)),
        out_shape=jax.ShapeDtypeStruct((m, d), jnp.float32),
        compiler_params=_cparams(("parallel",)),
        name="ffn",
    )(x2, gain, wg, wu, wd, fgain)


def _rope_kernel(pos_ref, freq_ref, m1_ref, m2_ref, c_ref, s1_ref, s2_ref):
    ang = pos_ref[...].astype(jnp.float32) * freq_ref[...]
    c = jnp.cos(ang)
    s = jnp.sin(ang)
    c_ref[...] = c
    s1_ref[...] = s * m1_ref[...]
    s2_ref[...] = s * m2_ref[...]


def _rope_tables(positions):
    s = positions.shape[0]
    freqs = ROPE_THETA ** (-jnp.arange(0, ROPE_DIM, 2, dtype=jnp.float32) / ROPE_DIM)
    half = ROPE_DIM // 2
    d = np.arange(LANES) % HEAD_DIM
    freq_row = jnp.where(d < ROPE_DIM, freqs[d % half], 0.0).reshape(1, LANES)
    m1 = jnp.asarray(np.where(d < half, -1.0, 0.0).astype(np.float32)).reshape(1, LANES)
    m2 = jnp.asarray(np.where((d >= half) & (d < ROPE_DIM), 1.0, 0.0).astype(np.float32)).reshape(1, LANES)
    tab = jax.ShapeDtypeStruct((s, LANES), jnp.float32)
    ts = min(s, 1024)
    row = pl.BlockSpec((1, LANES), lambda i: (0, 0))
    blk = pl.BlockSpec((ts, LANES), lambda i: (i, 0))
    return pl.pallas_call(
        _rope_kernel,
        grid=(s // ts,),
        in_specs=[pl.BlockSpec((ts, 1), lambda i: (i, 0)), row, row, row],
        out_specs=[blk, blk, blk],
        out_shape=[tab, tab, tab],
        compiler_params=_cparams(("parallel",)),
        name="rope_tables",
    )(positions.reshape(s, 1), freq_row, m1, m2)


def _proj_kernel(x_ref, g_ref, w_ref, wsm_ref, bsm_ref, c_ref, s1_ref, s2_ref, p_ref, sm_ref):
    hb = _rms(x_ref[...], g_ref[...], NORM_EPS).astype(jnp.bfloat16)
    c, s1, s2 = c_ref[...], s1_ref[...], s2_ref[...]
    chunk = 4
    for cb in range(N_PROJ_BLKS // chunk):
        y = _dot(hb, w_ref[:, cb * chunk * LANES:(cb + 1) * chunk * LANES])
        for sub in range(chunk):
            blk = cb * chunk + sub
            ys = y[:, sub * LANES:(sub + 1) * LANES]
            if blk in ROPE_BLKS:
                ys = (ys * c + pltpu.roll(ys, LANES - ROPE_DIM // 2, 1) * s1
                      + pltpu.roll(ys, ROPE_DIM // 2, 1) * s2)
            p_ref[:, blk * LANES:(blk + 1) * LANES] = ys.astype(jnp.bfloat16)
    sm_ref[...] = _dot(hb, wsm_ref[...]) + bsm_ref[...]


def _proj(x2, gain, w, wsm, bsm, c, s1, s2, seq, *, tm=512):
    m, d = x2.shape
    tm = min(tm, seq)
    nrep = seq // tm
    tab = pl.BlockSpec((tm, LANES), lambda i: (i % nrep, 0))
    return pl.pallas_call(
        _proj_kernel,
        grid=(m // tm,),
        in_specs=[
            pl.BlockSpec((tm, d), lambda i: (i, 0)),
            _resident((1, d)),
            _resident((d, PROJ_W)),
            _resident((d, LANES)),
            _resident((1, LANES)),
            tab, tab, tab,
        ],
        out_specs=[pl.BlockSpec((tm, PROJ_W), lambda i: (i, 0)),
                   pl.BlockSpec((tm, LANES), lambda i: (i, 0))],
        out_shape=[jax.ShapeDtypeStruct((m, PROJ_W), jnp.bfloat16),
                   jax.ShapeDtypeStruct((m, LANES), jnp.float32)],
        compiler_params=_cparams(("parallel",)),
        name="proj",
    )(x2, gain, w, wsm, bsm, c, s1, s2)


def _cumsum_kernel(sm_ref, cn_ref, ct_ref):
    t = sm_ref[0].T
    n = t.shape[1]
    ls = jnp.minimum(t, 0.0) - jnp.log(1.0 + jnp.exp(-jnp.abs(t)))
    lane = lax.broadcasted_iota(jnp.int32, ls.shape, 1)
    sh = 1
    while sh < n:
        ls = ls + jnp.where(lane >= sh, pltpu.roll(ls, sh, 1), 0.0)
        sh *= 2
    ls = ls * LOG2E
    ct_ref[0] = ls[SM_F:SM_F + 8, :]
    for h in range(4):
        cn_ref[0, h] = jnp.broadcast_to(ls[SM_F + h:SM_F + h + 1, :], ls.shape).T


def _cumsum(small3):
    b, s, _ = small3.shape
    return pl.pallas_call(
        _cumsum_kernel,
        grid=(b,),
        in_specs=[pl.BlockSpec((1, s, LANES), lambda i: (i, 0, 0))],
        out_specs=[pl.BlockSpec((1, 4, s, LANES), lambda i: (i, 0, 0, 0)),
                   pl.BlockSpec((1, 8, s), lambda i: (i, 0, 0))],
        out_shape=[jax.ShapeDtypeStruct((b, 4, s, LANES), jnp.float32),
                   jax.ShapeDtypeStruct((b, 8, s), jnp.float32)],
        compiler_params=_cparams(("parallel",)),
        name="forget_cumsum",
    )(small3)


def _half_masks(dtype):
    lane = lax.broadcasted_iota(jnp.int32, (1, LANES), 1)
    lo = (lane < HEAD_DIM).astype(dtype)
    return lo, (1 - lo).astype(dtype)


def _causal_mask(tq, tk, strict):
    r = lax.broadcasted_iota(jnp.int32, (tq, tk), 0)
    c = lax.broadcasted_iota(jnp.int32, (tq, tk), 1)
    return (c < r) if strict else (c <= r)


def _softmax2_step(s, m, l):
    m_new = jnp.maximum(m, jnp.max(s, axis=-1, keepdims=True))
    alpha = jnp.exp2(m - m_new)
    p = jnp.exp2(s - m_new)
    l_new = alpha * l + jnp.sum(p, axis=-1, keepdims=True)
    return m_new, l_new, alpha, p


def _softmax_step(s, m, l):
    m_new = jnp.maximum(m, jnp.max(s, axis=-1, keepdims=True))
    alpha = jnp.exp(m - m_new)
    p = jnp.exp(s - m_new)
    l_new = alpha * l + jnp.sum(p, axis=-1, keepdims=True)
    return m_new, l_new, alpha, p


def _build_vt(v_ref, vt_ref, seq, chunk=512):
    chunk = min(chunk, seq)
    for c in range(seq // chunk):
        blk = v_ref[0, c * chunk:(c + 1) * chunk, :].astype(jnp.float32)
        vt_ref[:, c * chunk:(c + 1) * chunk] = blk.T.astype(vt_ref.dtype)


def _diff_kernel(q_ref, k_ref, v_ref, lam_ref, gain_ref, o_ref,
                 vt_ref, acc_ref, m_ref, l_ref, sa_ref, sb_ref, *, t, lam_init, seq):
    qi = pl.program_id(2)

    @pl.when(qi == 0)
    def _():
        _build_vt(v_ref, vt_ref, seq)

    lo, hi = _half_masks(jnp.float32)
    q = q_ref[0].astype(jnp.float32) * (SCALE * LOG2E)
    qs = jnp.concatenate([(q * lo).astype(jnp.bfloat16), (q * hi).astype(jnp.bfloat16)], axis=0)
    acc_ref[...] = jnp.zeros_like(acc_ref)
    m_ref[...] = jnp.full(m_ref.shape, NEG_BIG, jnp.float32)
    l_ref[...] = jnp.zeros_like(l_ref)

    def logits(j, buf):
        buf[...] = _dot_nt(k_ref[0, pl.ds(pl.multiple_of(j * t, t), t), :], qs)

    def consume(j, buf, masked):
        s = buf[...]
        if masked:
            kr = lax.broadcasted_iota(jnp.int32, (t, 2 * t), 0)
            qc = lax.broadcasted_iota(jnp.int32, (t, 2 * t), 1)
            qc = jnp.where(qc >= t, qc - t, qc)
            s = jnp.where(kr <= qc, s, -jnp.inf)
        m = m_ref[...]
        m_new = jnp.maximum(m, jnp.max(s, axis=0, keepdims=True))
        alpha = jnp.exp2(m - m_new)
        p = jnp.exp2(s - m_new)
        l_ref[...] = alpha * l_ref[...] + jnp.sum(p, axis=0, keepdims=True)
        m_ref[...] = m_new
        vt = vt_ref[:, pl.ds(pl.multiple_of(j * t, t), t)]
        acc_ref[...] = alpha * acc_ref[...] + _dot(vt, p.astype(jnp.bfloat16))

    def pair(i, _):
        j = 2 * i
        logits(j + 1, sb_ref)
        consume(j, sa_ref, False)
        logits(j + 2, sa_ref)
        consume(j + 1, sb_ref, False)
        return 0

    logits(0, sa_ref)
    lax.fori_loop(0, qi // 2, pair, 0)

    @pl.when(qi % 2 == 0)
    def _():
        consume(qi, sa_ref, True)

    @pl.when(qi % 2 == 1)
    def _():
        logits(qi, sb_ref)
        consume(qi - 1, sa_ref, False)
        consume(qi, sb_ref, True)

    lp = lam_ref[...]
    lam = (jnp.exp(jnp.sum(lp[0:1] * lp[1:2], axis=-1, keepdims=True))
           - jnp.exp(jnp.sum(lp[2:3] * lp[3:4], axis=-1, keepdims=True)) + lam_init)
    yt = acc_ref[...] / l_ref[...]
    y = (yt[:, :t] - lam * yt[:, t:]).T
    y = _rms(y, gain_ref[...], SUBLN_EPS) * (1.0 - lam_init)
    o_ref[0] = y.astype(o_ref.dtype)


def _diff_attention(p3, lam_rows, gain, layer_idx, *, t=256):
    b, s, _ = p3.shape
    t = min(t, s)
    lam_init = 0.8 - 0.6 * math.exp(-0.3 * layer_idx)
    nh = 4
    return pl.pallas_call(
        functools.partial(_diff_kernel, t=t, lam_init=lam_init, seq=s),
        grid=(b, nh, s // t),
        in_specs=[
            pl.BlockSpec((1, t, LANES), lambda bi, h, qi: (bi, qi, BLK_AQ + h)),
            pl.BlockSpec((1, s, LANES), lambda bi, h, qi: (bi, 0, BLK_AK + h)),
            pl.BlockSpec((1, s, LANES), lambda bi, h, qi: (bi, 0, BLK_AV + h)),
            pl.BlockSpec((8, LANES), lambda bi, h, qi: (0, 0)),
            pl.BlockSpec((1, LANES), lambda bi, h, qi: (0, 0)),
        ],
        out_specs=pl.BlockSpec((1, t, LANES), lambda bi, h, qi: (bi, qi, h)),
        out_shape=jax.ShapeDtypeStruct((b, s, nh * LANES), jnp.bfloat16),
        scratch_shapes=[pltpu.VMEM((LANES, s), jnp.bfloat16),
                        pltpu.VMEM((LANES, 2 * t), jnp.float32),
                        pltpu.VMEM((1, 2 * t), jnp.float32), pltpu.VMEM((1, 2 * t), jnp.float32),
                        pltpu.VMEM((t, 2 * t), jnp.float32), pltpu.VMEM((t, 2 * t), jnp.float32)],
        compiler_params=_cparams(("arbitrary", "arbitrary", "arbitrary")),
        name="diff_attn",
    )(p3, p3, p3, lam_rows, gain)


def _stack_heads(q, scale):
    lo, hi = _half_masks(jnp.float32)
    q = q.astype(jnp.float32) * scale
    return jnp.concatenate([(q * lo).astype(jnp.bfloat16), (q * hi).astype(jnp.bfloat16)], axis=0)


def _stacked_causal(t, strict):
    kr = lax.broadcasted_iota(jnp.int32, (t, 2 * t), 0)
    qc = lax.broadcasted_iota(jnp.int32, (t, 2 * t), 1)
    qc = jnp.where(qc >= t, qc - t, qc)
    return (kr < qc) if strict else (kr <= qc)


def _unstack_heads_t(acc, t):
    return jnp.concatenate([acc[:HEAD_DIM, :t], acc[HEAD_DIM:, t:]], axis=0).T


def _fox_kernel(q_ref, k_ref, v_ref, cr_ref, ct_ref, o_ref,
                vt_ref, acc_ref, m_ref, l_ref, sa_ref, sb_ref, *, t, seq):
    pair = pl.program_id(1)
    qi = pl.program_id(2)

    @pl.when(qi == 0)
    def _():
        _build_vt(v_ref, vt_ref, seq)

    qs = _stack_heads(q_ref[0], SCALE * LOG2E)
    qoff = pl.multiple_of(qi * t, t)
    cq = jnp.concatenate([ct_ref[0, pl.ds(2 * pair + i, 1), pl.ds(qoff, t)] for i in range(2)], axis=1)
    acc_ref[...] = jnp.zeros_like(acc_ref)
    m_ref[...] = jnp.full(m_ref.shape, NEG_BIG, jnp.float32)
    l_ref[...] = jnp.zeros_like(l_ref)

    def logits(j, buf):
        buf[...] = _dot_nt(k_ref[0, pl.ds(pl.multiple_of(j * t, t), t), :], qs)

    def consume(j, buf, masked):
        off = pl.multiple_of(j * t, t)
        ck = jnp.concatenate([jnp.tile(cr_ref[0, i, pl.ds(off, t), :], (1, t // LANES)) for i in range(2)],
                             axis=1)
        s = (buf[...] + cq) - ck
        if masked:
            s = jnp.where(_stacked_causal(t, False), s, -jnp.inf)
        m = m_ref[...]
        m_new = jnp.maximum(m, jnp.max(s, axis=0, keepdims=True))
        alpha = jnp.exp2(m - m_new)
        p = jnp.exp2(s - m_new)
        l_ref[...] = alpha * l_ref[...] + jnp.sum(p, axis=0, keepdims=True)
        m_ref[...] = m_new
        acc_ref[...] = alpha * acc_ref[...] + _dot(vt_ref[:, pl.ds(off, t)], p.astype(jnp.bfloat16))

    def pair_step(i, _):
        j = 2 * i
        logits(j + 1, sb_ref)
        consume(j, sa_ref, False)
        logits(j + 2, sa_ref)
        consume(j + 1, sb_ref, False)
        return 0

    logits(0, sa_ref)
    lax.fori_loop(0, qi // 2, pair_step, 0)

    @pl.when(qi % 2 == 0)
    def _():
        consume(qi, sa_ref, True)

    @pl.when(qi % 2 == 1)
    def _():
        logits(qi, sb_ref)
        consume(qi - 1, sa_ref, False)
        consume(qi, sb_ref, True)

    o_ref[0] = _unstack_heads_t(acc_ref[...] / l_ref[...], t).astype(o_ref.dtype)


def _flash_t_scratch(s, t):
    return [pltpu.VMEM((LANES, s), jnp.bfloat16),
            pltpu.VMEM((LANES, 2 * t), jnp.float32),
            pltpu.VMEM((1, 2 * t), jnp.float32), pltpu.VMEM((1, 2 * t), jnp.float32),
            pltpu.VMEM((t, 2 * t), jnp.float32), pltpu.VMEM((t, 2 * t), jnp.float32)]


def _fox_attention(p3, cum_rep, cum_t, *, t=256):
    b, s, _ = p3.shape
    t = min(t, s)
    return pl.pallas_call(
        functools.partial(_fox_kernel, t=t, seq=s),
        grid=(b, 2, s // t),
        in_specs=[
            pl.BlockSpec((1, t, LANES), lambda bi, p, qi: (bi, qi, BLK_BQ + p)),
            pl.BlockSpec((1, s, LANES), lambda bi, p, qi: (bi, 0, BLK_BK + p)),
            pl.BlockSpec((1, s, LANES), lambda bi, p, qi: (bi, 0, BLK_BV + p)),
            pl.BlockSpec((1, 2, s, LANES), lambda bi, p, qi: (bi, p, 0, 0)),
            pl.BlockSpec((1, 8, s), lambda bi, p, qi: (bi, 0, 0)),
        ],
        out_specs=pl.BlockSpec((1, t, LANES), lambda bi, p, qi: (bi, qi, p)),
        out_shape=jax.ShapeDtypeStruct((b, s, 2 * LANES), jnp.bfloat16),
        scratch_shapes=_flash_t_scratch(s, t),
        compiler_params=_cparams(("arbitrary", "arbitrary", "arbitrary")),
        name="fox_attn",
    )(p3, p3, p3, cum_rep, cum_t)


def _sb_kernel(q_ref, k_ref, v_ref, o_ref, vt_ref, acc_ref, run_ref, sa_ref, sb_ref, *, t, seq):
    qi = pl.program_id(2)

    @pl.when(qi == 0)
    def _():
        _build_vt(v_ref, vt_ref, seq)

    qs = _stack_heads(q_ref[0], SCALE * LOG2E)
    r = lax.broadcasted_iota(jnp.int32, (t, t), 0)
    c = lax.broadcasted_iota(jnp.int32, (t, t), 1)
    later = (c > r).astype(jnp.bfloat16)
    acc_ref[...] = jnp.zeros_like(acc_ref)
    run_ref[...] = jnp.zeros_like(run_ref)

    def logits(j, buf):
        buf[...] = _dot_nt(k_ref[0, pl.ds(pl.multiple_of(j * t, t), t), :], qs)

    def consume(j, buf, masked):
        z = buf[...]
        log_beta = jnp.minimum(z, 0.0) - jnp.log2(1.0 + jnp.exp2(-jnp.abs(z)))
        l1m = log_beta - z
        if masked:
            cm = _stacked_causal(t, True)
            l1m = jnp.where(cm, l1m, 0.0)
        l_hi = l1m.astype(jnp.bfloat16)
        l_lo = (l1m - l_hi.astype(jnp.float32)).astype(jnp.bfloat16)
        after = _dot(later, l_hi) + _dot(later, l_lo) + run_ref[...]
        a = jnp.exp2(log_beta + after)
        if masked:
            a = jnp.where(cm, a, 0.0)
        vt = vt_ref[:, pl.ds(pl.multiple_of(j * t, t), t)]
        acc_ref[...] += _dot(vt, a.astype(jnp.bfloat16))
        run_ref[...] += jnp.sum(l1m, axis=0, keepdims=True)

    logits(qi, sa_ref)
    logits(jnp.maximum(qi - 1, 0), sb_ref)
    consume(qi, sa_ref, True)

    def pair_step(i, _):
        j = qi - 1 - 2 * i
        logits(j - 1, sa_ref)
        consume(j, sb_ref, False)
        logits(jnp.maximum(j - 2, 0), sb_ref)
        consume(j - 1, sa_ref, False)
        return 0

    lax.fori_loop(0, qi // 2, pair_step, 0)

    @pl.when(qi % 2 == 1)
    def _():
        consume(0, sb_ref, False)

    o_ref[0] = _unstack_heads_t(acc_ref[...], t).astype(o_ref.dtype)


def _sb_attention(p3, *, t=256):
    b, s, _ = p3.shape
    t = min(t, s)
    return pl.pallas_call(
        functools.partial(_sb_kernel, t=t, seq=s),
        grid=(b, 2, s // t),
        in_specs=[
            pl.BlockSpec((1, t, LANES), lambda bi, p, qi: (bi, qi, BLK_CQ + p)),
            pl.BlockSpec((1, s, LANES), lambda bi, p, qi: (bi, 0, BLK_CK + p)),
            pl.BlockSpec((1, s, LANES), lambda bi, p, qi: (bi, 0, BLK_CV + p)),
        ],
        out_specs=pl.BlockSpec((1, t, LANES), lambda bi, p, qi: (bi, qi, p)),
        out_shape=jax.ShapeDtypeStruct((b, s, 2 * LANES), jnp.bfloat16),
        scratch_shapes=[pltpu.VMEM((LANES, s), jnp.bfloat16), pltpu.VMEM((LANES, 2 * t), jnp.float32),
                        pltpu.VMEM((1, 2 * t), jnp.float32),
                        pltpu.VMEM((t, 2 * t), jnp.float32), pltpu.VMEM((t, 2 * t), jnp.float32)],
        compiler_params=_cparams(("arbitrary", "arbitrary", "arbitrary")),
        name="sb_attn",
    )(p3, p3, p3)


def _dsa_kernel(iq_ref, q_ref, sm_ref, ik_ref, k_ref, v_ref, o_ref,
                keys_ref, bias_ref, vt_ref, sig_ref, acc_ref, m_ref, l_ref, sa_ref, sb_ref,
                *, tq, ck, topk, seq):
    qi = pl.program_id(1)
    nblk = (qi * tq + tq + ck - 1) // ck
    lo, hi = _half_masks(jnp.float32)
    halves = (lo, hi)

    @pl.when(qi == 0)
    def _():
        _build_vt(v_ref, vt_ref, seq)

    iw_t = sm_ref[0].T[SM_IW:SM_IW + N_IDX_HEADS, :]
    iq = iq_ref[0].astype(jnp.float32)
    iq_stack = jnp.concatenate(
        [(iq[:, (h // 2) * LANES:(h // 2 + 1) * LANES] * halves[h % 2]).astype(jnp.bfloat16)
         for h in range(N_IDX_HEADS)], axis=0)
    qds = [_stack_heads(q_ref[0][:, pr * LANES:(pr + 1) * LANES], SCALE * LOG2E) for pr in range(2)]
    t_idx = qi * tq + lax.broadcasted_iota(jnp.int32, (1, tq), 1)
    row = lax.broadcasted_iota(jnp.int32, (ck, tq), 0)

    def score_chunk(c, _):
        off = pl.multiple_of(c * ck, ck)
        ikc = ik_ref[0, pl.ds(off, ck), :]
        rel = _dot_nt(ikc, iq_stack)
        acc = jnp.zeros((ck, tq), jnp.float32)
        for h in range(N_IDX_HEADS):
            acc = acc + iw_t[h:h + 1, :] * jnp.maximum(rel[:, h * tq:(h + 1) * tq], 0.0)
        bits = pltpu.bitcast(acc, jnp.int32)
        key = bits ^ ((bits >> 31) & 0x7FFFFFFF)
        keys_ref[pl.ds(off, ck), :] = jnp.where(row + off <= t_idx, key, INT_MIN)
        return 0

    lax.fori_loop(0, nblk, score_chunk, 0)

    def count(pred_fn):
        def body(c, cnt):
            off = pl.multiple_of(c * ck, ck)
            hit = pred_fn(keys_ref[pl.ds(off, ck), :], off).astype(jnp.int32)
            return cnt + jnp.sum(hit.reshape(ck // 8, 8, tq), axis=0)
        cnt8 = lax.fori_loop(0, nblk, body, jnp.zeros((8, tq), jnp.int32))
        return jnp.sum(cnt8, axis=0, keepdims=True)

    c0 = count(lambda kk, off: kk >= 0)
    tau0 = jnp.where(c0 >= topk, 0, INT_MIN).astype(jnp.int32)

    def bit_step(i, tau):
        cand = tau + jnp.left_shift(jnp.int32(1), 30 - i)
        cnt = count(lambda kk, off: kk >= cand)
        return jnp.where(cnt >= topk, cand, tau)

    tau = lax.fori_loop(0, 31, bit_step, tau0)
    cnt_ge = count(lambda kk, off: kk >= tau)
    real = tau != INT_MIN
    sig_ref[...] = jnp.where(real, seq, -1).astype(jnp.int32)
    over = jnp.max(jnp.where(real & (cnt_ge > topk), 1, 0))

    @pl.when(over > 0)
    def _():
        cnt_gt = count(lambda kk, off: kk > tau)
        need = topk - cnt_gt

        def idx_step(i, x):
            cand = x + jnp.left_shift(jnp.int32(1), (seq.bit_length() - 2) - i)
            f = count(lambda kk, off: (kk == tau) & (row + off < cand))
            return jnp.where(f < need, cand, x)

        x = lax.fori_loop(0, seq.bit_length() - 1, idx_step, jnp.zeros((1, tq), jnp.int32))
        sig_ref[...] = jnp.where(real, x, -1)

    sigma = sig_ref[...]

    def bias_chunk(c, _):
        off = pl.multiple_of(c * ck, ck)
        kk = keys_ref[pl.ds(off, ck), :]
        sel = (kk > tau) | ((kk == tau) & (row + off <= sigma))
        bias_ref[pl.ds(off, ck), :] = jnp.where(sel, 0.0, -jnp.inf)
        return 0

    lax.fori_loop(0, nblk, bias_chunk, 0)

    acc_ref[...] = jnp.zeros_like(acc_ref)
    m_ref[...] = jnp.full(m_ref.shape, NEG_BIG, jnp.float32)
    l_ref[...] = jnp.zeros_like(l_ref)

    def logits(c, buf):
        off = pl.multiple_of(c * ck, ck)
        for pr in range(2):
            buf[:, 2 * pr * tq:2 * (pr + 1) * tq] = _dot_nt(
                k_ref[0, pl.ds(off, ck), pr * LANES:(pr + 1) * LANES], qds[pr])

    def consume(c, buf):
        off = pl.multiple_of(c * ck, ck)
        s = buf[...] + jnp.tile(bias_ref[pl.ds(off, ck), :], (1, 4))
        m = m_ref[...]
        m_new = jnp.maximum(m, jnp.max(s, axis=0, keepdims=True))
        alpha = jnp.exp2(m - m_new)
        p = jnp.exp2(s - m_new)
        l_ref[...] = alpha * l_ref[...] + jnp.sum(p, axis=0, keepdims=True)
        m_ref[...] = m_new
        p = p.astype(jnp.bfloat16)
        pv = jnp.concatenate(
            [_dot(vt_ref[h * HEAD_DIM:(h + 1) * HEAD_DIM, pl.ds(off, ck)], p[:, h * tq:(h + 1) * tq])
             for h in range(4)], axis=1)
        acc_ref[...] = alpha * acc_ref[...] + pv

    def pair_step(i, _):
        c = 2 * i
        logits(c + 1, sb_ref)
        consume(c, sa_ref)
        logits(jnp.minimum(c + 2, nblk - 1), sa_ref)
        consume(c + 1, sb_ref)
        return 0

    logits(0, sa_ref)
    lax.fori_loop(0, nblk // 2, pair_step, 0)

    @pl.when(nblk % 2 == 1)
    def _():
        consume(nblk - 1, sa_ref)

    out = acc_ref[...] / l_ref[...]
    out_t = jnp.concatenate([out[:, h * tq:(h + 1) * tq] for h in range(4)], axis=0)
    o_ref[0] = out_t.T.astype(o_ref.dtype)


def _dsa_attention(p3, small3, *, tq=128, ck=512):
    b, s, _ = p3.shape
    ck = min(ck, s)
    topk = min(TOPK_MAX, s // 4)
    return pl.pallas_call(
        functools.partial(_dsa_kernel, tq=tq, ck=ck, topk=topk, seq=s),
        grid=(b, s // tq),
        in_specs=[
            pl.BlockSpec((1, tq, 4 * LANES), lambda bi, qi: (bi, qi, BLK_IQ // 4)),
            pl.BlockSpec((1, tq, 2 * LANES), lambda bi, qi: (bi, qi, BLK_DQ // 2)),
            pl.BlockSpec((1, tq, LANES), lambda bi, qi: (bi, qi, 0)),
            pl.BlockSpec((1, s, LANES), lambda bi, qi: (bi, 0, BLK_IK)),
            pl.BlockSpec((1, s, 2 * LANES), lambda bi, qi: (bi, 0, BLK_DK // 2)),
            pl.BlockSpec((1, s, 2 * LANES), lambda bi, qi: (bi, 0, BLK_DV // 2)),
        ],
        out_specs=pl.BlockSpec((1, tq, 2 * LANES), lambda bi, qi: (bi, qi, 0)),
        out_shape=jax.ShapeDtypeStruct((b, s, 2 * LANES), jnp.bfloat16),
        scratch_shapes=[
            pltpu.VMEM((s, tq), jnp.int32),
            pltpu.VMEM((s, tq), jnp.float32),
            pltpu.VMEM((4 * HEAD_DIM, s), jnp.bfloat16),
            pltpu.VMEM((1, tq), jnp.int32),
            pltpu.VMEM((HEAD_DIM, 4 * tq), jnp.float32),
            pltpu.VMEM((1, 4 * tq), jnp.float32), pltpu.VMEM((1, 4 * tq), jnp.float32),
            pltpu.VMEM((ck, 4 * tq), jnp.float32), pltpu.VMEM((ck, 4 * tq), jnp.float32),
        ],
        compiler_params=_cparams(("arbitrary", "arbitrary")),
        name="dsa_attn",
    )(p3, p3, small3, p3, p3, p3)


def _merge_kernel(x_ref, g_ref, wgate_ref, bgate_ref, ya_ref, yb_ref, yc_ref, yd_ref,
                  wa_ref, wb_ref, wc_ref, wd_ref, wo_ref, o_ref):
    x = x_ref[...]
    d = x.shape[1]
    hb = _rms(x, g_ref[...], NORM_EPS).astype(jnp.bfloat16)
    merged = jnp.zeros(x.shape, jnp.float32)
    for i, (y_ref, w_ref) in enumerate(((ya_ref, wa_ref), (yb_ref, wb_ref), (yc_ref, wc_ref), (yd_ref, wd_ref))):
        gate = jax.nn.sigmoid(_dot(hb, wgate_ref[:, i * d:(i + 1) * d]) + bgate_ref[:, i * d:(i + 1) * d])
        merged = merged + gate * _dot(y_ref[...], w_ref[...])
    o_ref[...] = x + _dot(merged.astype(jnp.bfloat16), wo_ref[...])


def _merge(x2, gain, wgate, bgate, ya, yb, yc, yd, wa, wb, wc, wd, wo, *, tm=512):
    m, d = x2.shape
    rowblk = lambda w: pl.BlockSpec((tm, w), lambda i: (i, 0))
    return pl.pallas_call(
        _merge_kernel,
        grid=(m // tm,),
        in_specs=[
            rowblk(d), _resident((1, d)), _resident((d, N_BRANCH * d)), _resident((1, N_BRANCH * d)),
            rowblk(ya.shape[1]), rowblk(yb.shape[1]), rowblk(yc.shape[1]), rowblk(yd.shape[1]),
            _resident(wa.shape), _resident(wb.shape), _resident(wc.shape), _resident(wd.shape),
            _resident(wo.shape),
        ],
        out_specs=rowblk(d),
        out_shape=jax.ShapeDtypeStruct((m, d), jnp.float32),
        compiler_params=_cparams(("parallel",)),
        name="merge",
    )(x2, gain, wgate, bgate, ya, yb, yc, yd, wa, wb, wc, wd, wo)


def _proj_weights(w_in, b_fgt):
    d = w_in.shape[0]
    o = 0

    def take(width):
        nonlocal o
        seg = w_in[:, o:o + width]
        o += width
        return seg

    aq, ak, av = take(512), take(512), take(512)
    bq, bk, bv, bf = take(256), take(256), take(256), take(4)
    cq, ck, cv = take(256), take(256), take(256)
    dq, dk, dv = take(256), take(256), take(256)
    diq, dik, diw = take(512), take(64), take(8)
    zeros = jnp.zeros((d, LANES), w_in.dtype)
    big = jnp.concatenate([aq, ak, av, bq, bk, bv, cq, ck, cv, dq, dk, dv, dik, dik, zeros, diq], axis=1)
    small = jnp.concatenate([diw, bf, jnp.zeros((d, LANES - 12), w_in.dtype)], axis=1)
    bias = jnp.concatenate([jnp.zeros((8,), jnp.float32), b_fgt.astype(jnp.float32),
                            jnp.zeros((LANES - 12,), jnp.float32)]).reshape(1, LANES)
    return big.astype(jnp.bfloat16), small.astype(jnp.bfloat16), bias


def kernel(x, positions, ffn1_norm, ffn1_w_gu, ffn1_w_down, mix_norm, w_in, b_fgt, lam_q1, lam_k1, lam_q2, lam_k2, diff_gain, w_gate, b_gate, w_br_a, w_br_b, w_br_c, w_br_d, w_out, ffn2_norm, ffn2_w_gu, ffn2_w_down, final_norm):
    b, s, d = x.shape
    depth = w_in.shape[0]
    bf = jnp.bfloat16
    cos_t, sin1_t, sin2_t = _rope_tables(positions)
    fgain = final_norm.reshape(1, d)
    x2 = x.reshape(b * s, d)
    for l in range(depth):
        x2 = _ffn(x2, ffn1_norm[l].reshape(1, d), ffn1_w_gu[l][:, :D_FF].astype(bf),
                  ffn1_w_gu[l][:, D_FF:].astype(bf), ffn1_w_down[l].astype(bf), fgain, final_norm=False)
        wbig, wsmall, bsmall = _proj_weights(w_in[l], b_fgt[l])
        p2, small2 = _proj(x2, mix_norm[l].reshape(1, d), wbig, wsmall, bsmall, cos_t, sin1_t, sin2_t, s)
        p3 = p2.reshape(b, s, PROJ_W)
        small3 = small2.reshape(b, s, LANES)
        cum_n, cum_t = _cumsum(small3)
        lam_rows = jnp.zeros((8, LANES), jnp.float32)
        lam_rows = lam_rows.at[0:4, 0:HEAD_DIM].set(jnp.stack([lam_q1[l], lam_k1[l], lam_q2[l], lam_k2[l]]))
        ya = _diff_attention(p3, lam_rows, diff_gain[l].reshape(1, LANES), l)
        yb = _fox_attention(p3, cum_n, cum_t)
        yc = _sb_attention(p3)
        yd = _dsa_attention(p3, small3)
        x2 = _merge(x2, mix_norm[l].reshape(1, d), w_gate[l].astype(bf), b_gate[l].reshape(1, N_BRANCH * d),
                    ya.reshape(b * s, -1), yb.reshape(b * s, -1), yc.reshape(b * s, -1), yd.reshape(b * s, -1),
                    w_br_a[l].astype(bf), w_br_b[l].astype(bf), w_br_c[l].astype(bf), w_br_d[l].astype(bf),
                    w_out[l].astype(bf))
        x2 = _ffn(x2, ffn2_norm[l].reshape(1, d), ffn2_w_gu[l][:, :D_FF].astype(bf),
                  ffn2_w_gu[l][:, D_FF:].astype(bf), ffn2_w_down[l].astype(bf), fgain,
                  final_norm=(l == depth - 1))
    return x2.reshape(b, s, d)
```

```python
import functools
import math

import numpy as np
import jax
import jax.numpy as jnp
from jax import lax
from jax.experimental import pallas as pl
from jax.experimental.pallas import tpu as pltpu

D_MODEL = 1024
HEAD_DIM = 64
N_IDX_HEADS = 8
TOPK_MAX = 256
ROPE_THETA = 500000.0
ROPE_DIM = HEAD_DIM // 4
D_FF = 2816
N_BRANCH = 4
NORM_EPS = 1e-6
SUBLN_EPS = 1e-5
SCALE = HEAD_DIM ** -0.5
LOG2E = math.log2(math.e)

LANES = 128
VMEM_LIMIT = 56 * 1024 * 1024

BLK_AQ, BLK_AK, BLK_AV = 0, 4, 8
BLK_BQ, BLK_BK, BLK_BV = 12, 14, 16
BLK_CQ, BLK_CK, BLK_CV = 18, 20, 22
BLK_DQ, BLK_DK, BLK_DV = 24, 26, 28
BLK_IK, BLK_SPARE, BLK_IQ = 30, 31, 32
N_PROJ_BLKS = 36
PROJ_W = N_PROJ_BLKS * LANES
ROPE_BLKS = frozenset(list(range(0, 8)) + [24, 25, 26, 27, 30] + list(range(32, 36)))
SM_IW, SM_F = 0, 8

NEG_BIG = -1e30
INT_MIN = -(2 ** 31)


def _cparams(sem, vmem=VMEM_LIMIT):
    return pltpu.CompilerParams(dimension_semantics=sem, vmem_limit_bytes=vmem)


def _resident(shape):
    nd = len(shape)
    return pl.BlockSpec(shape, lambda *_: (0,) * nd, pipeline_mode=pl.Buffered(1))


def _rms(x, g, eps):
    ms = jnp.mean(x * x, axis=-1, keepdims=True)
    return x * lax.rsqrt(ms + eps) * g


def _dot(a, b):
    return jnp.dot(a, b, preferred_element_type=jnp.float32)


def _dot_nt(a, b):
    return lax.dot_general(a, b, (((1,), (1,)), ((), ())), preferred_element_type=jnp.float32)


def _ffn_kernel(x_ref, g_ref, wg_ref, wu_ref, wd_ref, fg_ref, o_ref, *, tf, final_norm):
    x = x_ref[...]
    hb = _rms(x, g_ref[...], NORM_EPS).astype(jnp.bfloat16)
    acc = jnp.zeros(x.shape, jnp.float32)
    for j in range(D_FF // tf):
        sl = slice(j * tf, (j + 1) * tf)
        g = _dot(hb, wg_ref[:, sl])
        u = _dot(hb, wu_ref[:, sl])
        a = (g * jax.nn.sigmoid(g) * u).astype(jnp.bfloat16)
        acc = acc + _dot(a, wd_ref[sl, :])
    y = x + 0.5 * acc
    if final_norm:
        y = _rms(y, fg_ref[...], NORM_EPS)
    o_ref[...] = y


def _ffn(x2, gain, wg, wu, wd, fgain, *, final_norm, tm=512, tf=256):
    m, d = x2.shape
    return pl.pallas_call(
        functools.partial(_ffn_kernel, tf=tf, final_norm=final_norm),
        grid=(m // tm,),
        in_specs=[
            pl.BlockSpec((tm, d), lambda i: (i, 0)),
            _resident((1, d)),
            _resident((d, D_FF)),
            _resident((d, D_FF)),
            _resident((D_FF, d)),
            _resident((1, d)),
        ],
        out_specs=pl.BlockSpec((tm, d), lambda i: (i, 0)),
        out_shape=jax.ShapeDtypeStruct((m, d), jnp.float32),
        compiler_params=_cparams(("parallel",)),
        name="ffn",
    )(x2, gain, wg, wu, wd, fgain)


def _rope_kernel(pos_ref, freq_ref, m1_ref, m2_ref, c_ref, s1_ref, s2_ref):
    ang = pos_ref[...].astype(jnp.float32) * freq_ref[...]
    c = jnp.cos(ang)
    s = jnp.sin(ang)
    c_ref[...] = c
    s1_ref[...] = s * m1_ref[...]
    s2_ref[...] = s * m2_ref[...]


def _rope_tables(positions):
    s = positions.shape[0]
    freqs = ROPE_THETA ** (-jnp.arange(0, ROPE_DIM, 2, dtype=jnp.float32) / ROPE_DIM)
    half = ROPE_DIM // 2
    d = np.arange(LANES) % HEAD_DIM
    freq_row = jnp.where(d < ROPE_DIM, freqs[d % half], 0.0).reshape(1, LANES)
    m1 = jnp.asarray(np.where(d < half, -1.0, 0.0).astype(np.float32)).reshape(1, LANES)
    m2 = jnp.asarray(np.where((d >= half) & (d < ROPE_DIM), 1.0, 0.0).astype(np.float32)).reshape(1, LANES)
    tab = jax.ShapeDtypeStruct((s, LANES), jnp.float32)
    ts = min(s, 1024)
    row = pl.BlockSpec((1, LANES), lambda i: (0, 0))
    blk = pl.BlockSpec((ts, LANES), lambda i: (i, 0))
    return pl.pallas_call(
        _rope_kernel,
        grid=(s // ts,),
        in_specs=[pl.BlockSpec((ts, 1), lambda i: (i, 0)), row, row, row],
        out_specs=[blk, blk, blk],
        out_shape=[tab, tab, tab],
        compiler_params=_cparams(("parallel",)),
        name="rope_tables",
    )(positions.reshape(s, 1), freq_row, m1, m2)


def _proj_kernel(x_ref, g_ref, w_ref, wsm_ref, bsm_ref, c_ref, s1_ref, s2_ref, p_ref, sm_ref):
    hb = _rms(x_ref[...], g_ref[...], NORM_EPS).astype(jnp.bfloat16)
    c, s1, s2 = c_ref[...], s1_ref[...], s2_ref[...]
    chunk = 4
    for cb in range(N_PROJ_BLKS // chunk):
        y = _dot(hb, w_ref[:, cb * chunk * LANES:(cb + 1) * chunk * LANES])
        for sub in range(chunk):
            blk = cb * chunk + sub
            ys = y[:, sub * LANES:(sub + 1) * LANES]
            if blk in ROPE_BLKS:
                ys = (ys * c + pltpu.roll(ys, LANES - ROPE_DIM // 2, 1) * s1
                      + pltpu.roll(ys, ROPE_DIM // 2, 1) * s2)
            p_ref[:, blk * LANES:(blk + 1) * LANES] = ys.astype(jnp.bfloat16)
    sm_ref[...] = _dot(hb, wsm_ref[...]) + bsm_ref[...]


def _proj(x2, gain, w, wsm, bsm, c, s1, s2, seq, *, tm=512):
    m, d = x2.shape
    tm = min(tm, seq)
    nrep = seq // tm
    tab = pl.BlockSpec((tm, LANES), lambda i: (i % nrep, 0))
    return pl.pallas_call(
        _proj_kernel,
        grid=(m // tm,),
        in_specs=[
            pl.BlockSpec((tm, d), lambda i: (i, 0)),
            _resident((1, d)),
            _resident((d, PROJ_W)),
            _resident((d, LANES)),
            _resident((1, LANES)),
            tab, tab, tab,
        ],
        out_specs=[pl.BlockSpec((tm, PROJ_W), lambda i: (i, 0)),
                   pl.BlockSpec((tm, LANES), lambda i: (i, 0))],
        out_shape=[jax.ShapeDtypeStruct((m, PROJ_W), jnp.bfloat16),
                   jax.ShapeDtypeStruct((m, LANES), jnp.float32)],
        compiler_params=_cparams(("parallel",)),
        name="proj",
    )(x2, gain, w, wsm, bsm, c, s1, s2)


def _cumsum_kernel(sm_ref, cn_ref, ct_ref):
    t = sm_ref[0].T
    n = t.shape[1]
    ls = jnp.minimum(t, 0.0) - jnp.log(1.0 + jnp.exp(-jnp.abs(t)))
    lane = lax.broadcasted_iota(jnp.int32, ls.shape, 1)
    sh = 1
    while sh < n:
        ls = ls + jnp.where(lane >= sh, pltpu.roll(ls, sh, 1), 0.0)
        sh *= 2
    ls = ls * LOG2E
    ct_ref[0] = ls[SM_F:SM_F + 8, :]
    for h in range(4):
        cn_ref[0, h] = jnp.broadcast_to(ls[SM_F + h:SM_F + h + 1, :], ls.shape).T


def _cumsum(small3):
    b, s, _ = small3.shape
    return pl.pallas_call(
        _cumsum_kernel,
        grid=(b,),
        in_specs=[pl.BlockSpec((1, s, LANES), lambda i: (i, 0, 0))],
        out_specs=[pl.BlockSpec((1, 4, s, LANES), lambda i: (i, 0, 0, 0)),
                   pl.BlockSpec((1, 8, s), lambda i: (i, 0, 0))],
        out_shape=[jax.ShapeDtypeStruct((b, 4, s, LANES), jnp.float32),
                   jax.ShapeDtypeStruct((b, 8, s), jnp.float32)],
        compiler_params=_cparams(("parallel",)),
        name="forget_cumsum",
    )(small3)


def _half_masks(dtype):
    lane = lax.broadcasted_iota(jnp.int32, (1, LANES), 1)
    lo = (lane < HEAD_DIM).astype(dtype)
    return lo, (1 - lo).astype(dtype)


def _causal_mask(tq, tk, strict):
    r = lax.broadcasted_iota(jnp.int32, (tq, tk), 0)
    c = lax.broadcasted_iota(jnp.int32, (tq, tk), 1)
    return (c < r) if strict else (c <= r)


def _softmax2_step(s, m, l):
    m_new = jnp.maximum(m, jnp.max(s, axis=-1, keepdims=True))
    alpha = jnp.exp2(m - m_new)
    p = jnp.exp2(s - m_new)
    l_new = alpha * l + jnp.sum(p, axis=-1, keepdims=True)
    return m_new, l_new, alpha, p


def _softmax_step(s, m, l):
    m_new = jnp.maximum(m, jnp.max(s, axis=-1, keepdims=True))
    alpha = jnp.exp(m - m_new)
    p = jnp.exp(s - m_new)
    l_new = alpha * l + jnp.sum(p, axis=-1, keepdims=True)
    return m_new, l_new, alpha, p


def _pipeline_masked_last(n, logits, softmax, pv):
    logits(0, 0)

    def body(i, _):
        s = 2 * i
        logits(s + 1, 1)
        softmax(s, 0, False)
        pv(jnp.maximum(s - 1, 0), 1)
        logits(s + 2, 0)
        softmax(s + 1, 1, False)
        pv(s, 0)
        return 0

    lax.fori_loop(0, n // 2, body, 0)

    @pl.when(n % 2 == 0)
    def _():
        pv(jnp.maximum(n - 1, 0), 1)
        softmax(n, 0, True)
        pv(n, 0)

    @pl.when(n % 2 == 1)
    def _():
        logits(n, 1)
        softmax(n - 1, 0, False)
        pv(jnp.maximum(n - 2, 0), 1)
        softmax(n, 1, True)
        pv(n - 1, 0)
        pv(n, 1)


def _pipeline_masked_first(n, logits, softmax, pv):
    logits(0, 0)
    logits(jnp.minimum(1, n), 1)
    softmax(0, 0, True)

    def body(i, _):
        s = 2 * i + 1
        logits(s + 1, 0)
        softmax(s, 1, False)
        pv(s - 1, 0)
        logits(jnp.minimum(s + 2, n), 1)
        softmax(s + 1, 0, False)
        pv(s, 1)
        return 0

    lax.fori_loop(0, n // 2, body, 0)

    @pl.when(n % 2 == 0)
    def _():
        pv(n, 0)

    @pl.when(n % 2 == 1)
    def _():
        softmax(n, 1, False)
        pv(n - 1, 0)
        pv(n, 1)


def _build_vt(v_ref, vt_ref, seq, chunk=512):
    chunk = min(chunk, seq)
    for c in range(seq // chunk):
        blk = v_ref[0, c * chunk:(c + 1) * chunk, :].astype(jnp.float32)
        vt_ref[:, c * chunk:(c + 1) * chunk] = blk.T.astype(vt_ref.dtype)


def _softmax_stage(s, m_ref, l_ref, p_buf, a_buf):
    m = m_ref[...]
    m_new = jnp.maximum(m, jnp.max(s, axis=0, keepdims=True))
    alpha = jnp.exp2(m - m_new)
    p = jnp.exp2(s - m_new)
    l_ref[...] = alpha * l_ref[...] + jnp.sum(p, axis=0, keepdims=True)
    m_ref[...] = m_new
    a_buf[...] = alpha
    p_buf[...] = p.astype(jnp.bfloat16)


def _init_softmax_state(acc_ref, m_ref, l_ref, p_bufs, a_bufs):
    acc_ref[...] = jnp.zeros_like(acc_ref)
    m_ref[...] = jnp.full(m_ref.shape, NEG_BIG, jnp.float32)
    l_ref[...] = jnp.zeros_like(l_ref)
    p_bufs[1][...] = jnp.zeros_like(p_bufs[1])
    a_bufs[1][...] = jnp.ones_like(a_bufs[1])


def _diff_kernel(q_ref, k_ref, v_ref, lam_ref, gain_ref, o_ref,
                 vt_ref, acc_ref, m_ref, l_ref, s0_ref, s1_ref, p0_ref, p1_ref, a0_ref, a1_ref,
                 *, t, lam_init, seq):
    qi = pl.program_id(2)
    s_bufs, p_bufs, a_bufs = (s0_ref, s1_ref), (p0_ref, p1_ref), (a0_ref, a1_ref)

    @pl.when(qi == 0)
    def _():
        _build_vt(v_ref, vt_ref, seq)

    qs = _stack_heads(q_ref[0], SCALE * LOG2E)
    _init_softmax_state(acc_ref, m_ref, l_ref, p_bufs, a_bufs)

    def logits(j, slot):
        s_bufs[slot][...] = _dot_nt(k_ref[0, pl.ds(pl.multiple_of(j * t, t), t), :], qs)

    def softmax(j, slot, masked):
        s = s_bufs[slot][...]
        if masked:
            s = jnp.where(_stacked_causal(t, False), s, -jnp.inf)
        _softmax_stage(s, m_ref, l_ref, p_bufs[slot], a_bufs[slot])

    def pv(j, slot):
        vt = vt_ref[:, pl.ds(pl.multiple_of(j * t, t), t)]
        acc_ref[...] = a_bufs[slot][...] * acc_ref[...] + _dot(vt, p_bufs[slot][...])

    _pipeline_masked_last(qi, logits, softmax, pv)

    lp = lam_ref[...]
    lam = (jnp.exp(jnp.sum(lp[0:1] * lp[1:2], axis=-1, keepdims=True))
           - jnp.exp(jnp.sum(lp[2:3] * lp[3:4], axis=-1, keepdims=True)) + lam_init)
    yt = acc_ref[...] / l_ref[...]
    y = (yt[:, :t] - lam * yt[:, t:]).T
    y = _rms(y, gain_ref[...], SUBLN_EPS) * (1.0 - lam_init)
    o_ref[0] = y.astype(o_ref.dtype)


def _diff_attention(p3, lam_rows, gain, layer_idx, *, t=256):
    b, s, _ = p3.shape
    t = min(t, s)
    lam_init = 0.8 - 0.6 * math.exp(-0.3 * layer_idx)
    nh = 4
    return pl.pallas_call(
        functools.partial(_diff_kernel, t=t, lam_init=lam_init, seq=s),
        grid=(b, nh, s // t),
        in_specs=[
            pl.BlockSpec((1, t, LANES), lambda bi, h, qi: (bi, qi, BLK_AQ + h)),
            pl.BlockSpec((1, s, LANES), lambda bi, h, qi: (bi, 0, BLK_AK + h)),
            pl.BlockSpec((1, s, LANES), lambda bi, h, qi: (bi, 0, BLK_AV + h)),
            pl.BlockSpec((8, LANES), lambda bi, h, qi: (0, 0)),
            pl.BlockSpec((1, LANES), lambda bi, h, qi: (0, 0)),
        ],
        out_specs=pl.BlockSpec((1, t, LANES), lambda bi, h, qi: (bi, qi, h)),
        out_shape=jax.ShapeDtypeStruct((b, s, nh * LANES), jnp.bfloat16),
        scratch_shapes=_flash_t_scratch(s, t),
        compiler_params=_cparams(("arbitrary", "arbitrary", "arbitrary")),
        name="diff_attn",
    )(p3, p3, p3, lam_rows, gain)


def _stack_heads(q, scale):
    lo, hi = _half_masks(jnp.float32)
    q = q.astype(jnp.float32) * scale
    return jnp.concatenate([(q * lo).astype(jnp.bfloat16), (q * hi).astype(jnp.bfloat16)], axis=0)


def _stacked_causal(t, strict):
    kr = lax.broadcasted_iota(jnp.int32, (t, 2 * t), 0)
    qc = lax.broadcasted_iota(jnp.int32, (t, 2 * t), 1)
    qc = jnp.where(qc >= t, qc - t, qc)
    return (kr < qc) if strict else (kr <= qc)


def _unstack_heads_t(acc, t):
    return jnp.concatenate([acc[:HEAD_DIM, :t], acc[HEAD_DIM:, t:]], axis=0).T


def _fox_kernel(q_ref, k_ref, v_ref, cr_ref, ct_ref, o_ref,
                vt_ref, acc_ref, m_ref, l_ref, s0_ref, s1_ref, p0_ref, p1_ref, a0_ref, a1_ref, *, t, seq):
    pair = pl.program_id(1)
    qi = pl.program_id(2)
    s_bufs, p_bufs, a_bufs = (s0_ref, s1_ref), (p0_ref, p1_ref), (a0_ref, a1_ref)

    @pl.when(qi == 0)
    def _():
        _build_vt(v_ref, vt_ref, seq)

    qs = _stack_heads(q_ref[0], SCALE * LOG2E)
    qoff = pl.multiple_of(qi * t, t)
    cq = jnp.concatenate([ct_ref[0, pl.ds(2 * pair + i, 1), pl.ds(qoff, t)] for i in range(2)], axis=1)
    _init_softmax_state(acc_ref, m_ref, l_ref, p_bufs, a_bufs)

    def logits(j, slot):
        s_bufs[slot][...] = _dot_nt(k_ref[0, pl.ds(pl.multiple_of(j * t, t), t), :], qs)

    def softmax(j, slot, masked):
        off = pl.multiple_of(j * t, t)
        ck = jnp.concatenate([jnp.tile(cr_ref[0, i, pl.ds(off, t), :], (1, t // LANES)) for i in range(2)],
                             axis=1)
        s = (s_bufs[slot][...] + cq) - ck
        if masked:
            s = jnp.where(_stacked_causal(t, False), s, -jnp.inf)
        _softmax_stage(s, m_ref, l_ref, p_bufs[slot], a_bufs[slot])

    def pv(j, slot):
        vt = vt_ref[:, pl.ds(pl.multiple_of(j * t, t), t)]
        acc_ref[...] = a_bufs[slot][...] * acc_ref[...] + _dot(vt, p_bufs[slot][...])

    _pipeline_masked_last(qi, logits, softmax, pv)

    o_ref[0] = _unstack_heads_t(acc_ref[...] / l_ref[...], t).astype(o_ref.dtype)


def _flash_t_scratch(s, t):
    row = pltpu.VMEM((1, 2 * t), jnp.float32)
    return [pltpu.VMEM((LANES, s), jnp.bfloat16),
            pltpu.VMEM((LANES, 2 * t), jnp.float32),
            row, row,
            pltpu.VMEM((t, 2 * t), jnp.float32), pltpu.VMEM((t, 2 * t), jnp.float32),
            pltpu.VMEM((t, 2 * t), jnp.bfloat16), pltpu.VMEM((t, 2 * t), jnp.bfloat16),
            row, row]


def _fox_attention(p3, cum_rep, cum_t, *, t=256):
    b, s, _ = p3.shape
    t = min(t, s)
    return pl.pallas_call(
        functools.partial(_fox_kernel, t=t, seq=s),
        grid=(b, 2, s // t),
        in_specs=[
            pl.BlockSpec((1, t, LANES), lambda bi, p, qi: (bi, qi, BLK_BQ + p)),
            pl.BlockSpec((1, s, LANES), lambda bi, p, qi: (bi, 0, BLK_BK + p)),
            pl.BlockSpec((1, s, LANES), lambda bi, p, qi: (bi, 0, BLK_BV + p)),
            pl.BlockSpec((1, 2, s, LANES), lambda bi, p, qi: (bi, p, 0, 0)),
            pl.BlockSpec((1, 8, s), lambda bi, p, qi: (bi, 0, 0)),
        ],
        out_specs=pl.BlockSpec((1, t, LANES), lambda bi, p, qi: (bi, qi, p)),
        out_shape=jax.ShapeDtypeStruct((b, s, 2 * LANES), jnp.bfloat16),
        scratch_shapes=_flash_t_scratch(s, t),
        compiler_params=_cparams(("arbitrary", "arbitrary", "arbitrary")),
        name="fox_attn",
    )(p3, p3, p3, cum_rep, cum_t)


def _sb_kernel(q_ref, k_ref, v_ref, o_ref, vt_ref, acc_ref, run_ref,
               s0_ref, s1_ref, p0_ref, p1_ref, *, t, seq):
    qi = pl.program_id(2)
    s_bufs, p_bufs = (s0_ref, s1_ref), (p0_ref, p1_ref)

    @pl.when(qi == 0)
    def _():
        _build_vt(v_ref, vt_ref, seq)

    qs = _stack_heads(q_ref[0], SCALE * LOG2E)
    r = lax.broadcasted_iota(jnp.int32, (t, 2 * t), 0)
    c = lax.broadcasted_iota(jnp.int32, (t, 2 * t), 1)
    later2 = (jnp.where(c >= t, c - t, c) > r).astype(jnp.bfloat16)
    acc_ref[...] = jnp.zeros_like(acc_ref)
    run_ref[...] = jnp.zeros_like(run_ref)

    def block_off(step):
        return pl.multiple_of((qi - step) * t, t)

    def logits(step, slot):
        s_bufs[slot][...] = _dot_nt(k_ref[0, pl.ds(block_off(step), t), :], qs)

    def weights(step, slot, masked):
        z = s_bufs[slot][...]
        log_beta = jnp.minimum(z, 0.0) - jnp.log2(1.0 + jnp.exp2(-jnp.abs(z)))
        l1m = log_beta - z
        if masked:
            cm = _stacked_causal(t, True)
            l1m = jnp.where(cm, l1m, 0.0)
        l_hi = l1m.astype(jnp.bfloat16)
        l_lo = (l1m - l_hi.astype(jnp.float32)).astype(jnp.bfloat16)
        after = _dot(later2, jnp.concatenate([l_hi, l_lo], axis=0)) + run_ref[...]
        a = jnp.exp2(log_beta + after)
        if masked:
            a = jnp.where(cm, a, 0.0)
        p_bufs[slot][...] = a.astype(jnp.bfloat16)
        run_ref[...] += jnp.sum(l1m, axis=0, keepdims=True)

    def pv(step, slot):
        acc_ref[...] += _dot(vt_ref[:, pl.ds(block_off(step), t)], p_bufs[slot][...])

    _pipeline_masked_first(qi, logits, weights, pv)

    o_ref[0] = _unstack_heads_t(acc_ref[...], t).astype(o_ref.dtype)


def _sb_attention(p3, *, t=256):
    b, s, _ = p3.shape
    t = min(t, s)
    return pl.pallas_call(
        functools.partial(_sb_kernel, t=t, seq=s),
        grid=(b, 2, s // t),
        in_specs=[
            pl.BlockSpec((1, t, LANES), lambda bi, p, qi: (bi, qi, BLK_CQ + p)),
            pl.BlockSpec((1, s, LANES), lambda bi, p, qi: (bi, 0, BLK_CK + p)),
            pl.BlockSpec((1, s, LANES), lambda bi, p, qi: (bi, 0, BLK_CV + p)),
        ],
        out_specs=pl.BlockSpec((1, t, LANES), lambda bi, p, qi: (bi, qi, p)),
        out_shape=jax.ShapeDtypeStruct((b, s, 2 * LANES), jnp.bfloat16),
        scratch_shapes=[pltpu.VMEM((LANES, s), jnp.bfloat16), pltpu.VMEM((LANES, 2 * t), jnp.float32),
                        pltpu.VMEM((1, 2 * t), jnp.float32),
                        pltpu.VMEM((t, 2 * t), jnp.float32), pltpu.VMEM((t, 2 * t), jnp.float32),
                        pltpu.VMEM((t, 2 * t), jnp.bfloat16), pltpu.VMEM((t, 2 * t), jnp.bfloat16)],
        compiler_params=_cparams(("arbitrary", "arbitrary", "arbitrary")),
        name="sb_attn",
    )(p3, p3, p3)


def _dsa_kernel(iq_ref, q_ref, sm_ref, ik_ref, k_ref, v_ref, o_ref,
                keys_ref, hi_ref, lo_ref, bias_ref, vt_ref, sig_ref, acc_ref, m_ref, l_ref,
                s0_ref, s1_ref, p0_ref, p1_ref, a0_ref, a1_ref,
                *, tq, ck, topk, seq):
    qi = pl.program_id(1)
    nblk = (qi * tq + tq + ck - 1) // ck
    lo, hi = _half_masks(jnp.float32)
    halves = (lo, hi)

    @pl.when(qi == 0)
    def _():
        _build_vt(v_ref, vt_ref, seq)

    iw_t = sm_ref[0].T[SM_IW:SM_IW + N_IDX_HEADS, :]
    iq = iq_ref[0].astype(jnp.float32)
    iq_stack = jnp.concatenate(
        [(iq[:, (h // 2) * LANES:(h // 2 + 1) * LANES] * halves[h % 2]).astype(jnp.bfloat16)
         for h in range(N_IDX_HEADS)], axis=0)
    qds = [_stack_heads(q_ref[0][:, pr * LANES:(pr + 1) * LANES], SCALE * LOG2E) for pr in range(2)]
    t_idx = qi * tq + lax.broadcasted_iota(jnp.int32, (1, tq), 1)
    row = lax.broadcasted_iota(jnp.int32, (ck, tq), 0)

    def score_chunk(c, _):
        off = pl.multiple_of(c * ck, ck)
        ikc = ik_ref[0, pl.ds(off, ck), :]
        rel = _dot_nt(ikc, iq_stack)
        acc = jnp.zeros((ck, tq), jnp.float32)
        for h in range(N_IDX_HEADS):
            acc = acc + iw_t[h:h + 1, :] * jnp.maximum(rel[:, h * tq:(h + 1) * tq], 0.0)
        bits = pltpu.bitcast(acc, jnp.int32)
        key = bits ^ ((bits >> 31) & 0x7FFFFFFF)
        key = jnp.where(row + off <= t_idx, key, INT_MIN)
        keys_ref[pl.ds(off, ck), :] = key
        hi_ref[pl.ds(off, ck), :] = (key >> 16).astype(jnp.int16)
        lo_ref[pl.ds(off, ck), :] = ((key & 0xFFFF) - 32768).astype(jnp.int16)
        return 0

    lax.fori_loop(0, nblk, score_chunk, 0)

    def count(pred_fn):
        def body(c, cnt):
            off = pl.multiple_of(c * ck, ck)
            hit = pred_fn(keys_ref[pl.ds(off, ck), :], off).astype(jnp.int32)
            return cnt + jnp.sum(hit.reshape(ck // 8, 8, tq), axis=0)
        cnt8 = lax.fori_loop(0, nblk, body, jnp.zeros((8, tq), jnp.int32))
        return jnp.sum(cnt8, axis=0, keepdims=True)

    def count16(ref, cand):
        cand16 = cand.astype(jnp.int16)

        def body(c, cnt):
            off = pl.multiple_of(c * ck, ck)
            hit = (ref[pl.ds(off, ck), :] >= cand16).astype(jnp.int16)
            parts = [hit[g * 16:(g + 1) * 16] for g in range(ck // 16)]
            while len(parts) > 1:
                parts = [a + b for a, b in zip(parts[::2], parts[1::2])]
            return cnt + parts[0]
        cnt16 = lax.fori_loop(0, nblk, body, jnp.zeros((16, tq), jnp.int16))
        return jnp.sum(cnt16.astype(jnp.int32), axis=0, keepdims=True)

    def search16(ref):
        c0 = count16(ref, jnp.zeros((1, tq), jnp.int32))
        x0 = jnp.where(c0 >= topk, 0, -32768).astype(jnp.int32)

        def bit_step(i, x):
            cand = x + jnp.left_shift(jnp.int32(1), 14 - i)
            return jnp.where(count16(ref, cand) >= topk, cand, x)

        return lax.fori_loop(0, 15, bit_step, x0)

    tau_hi = search16(hi_ref)
    tau_hi16 = tau_hi.astype(jnp.int16)

    def refine_chunk(c, _):
        off = pl.multiple_of(c * ck, ck)
        hi = hi_ref[pl.ds(off, ck), :]
        lo = lo_ref[pl.ds(off, ck), :]
        lo_ref[pl.ds(off, ck), :] = jnp.where(hi == tau_hi16, lo,
                                              jnp.where(hi > tau_hi16, jnp.int16(32767), jnp.int16(-32768)))
        return 0

    lax.fori_loop(0, nblk, refine_chunk, 0)
    tau = tau_hi * 65536 + (search16(lo_ref) + 32768)
    cnt_ge = count(lambda kk, off: kk >= tau)
    real = tau != INT_MIN
    sig_ref[...] = jnp.where(real, seq, -1).astype(jnp.int32)
    over = jnp.max(jnp.where(real & (cnt_ge > topk), 1, 0))

    @pl.when(over > 0)
    def _():
        cnt_gt = count(lambda kk, off: kk > tau)
        need = topk - cnt_gt

        def idx_step(i, x):
            cand = x + jnp.left_shift(jnp.int32(1), (seq.bit_length() - 2) - i)
            f = count(lambda kk, off: (kk == tau) & (row + off < cand))
            return jnp.where(f < need, cand, x)

        x = lax.fori_loop(0, seq.bit_length() - 1, idx_step, jnp.zeros((1, tq), jnp.int32))
        sig_ref[...] = jnp.where(real, x, -1)

    sigma = sig_ref[...]

    def bias_chunk(c, _):
        off = pl.multiple_of(c * ck, ck)
        kk = keys_ref[pl.ds(off, ck), :]
        sel = (kk > tau) | ((kk == tau) & (row + off <= sigma))
        bias_ref[pl.ds(off, ck), :] = jnp.where(sel, 0.0, -jnp.inf)
        return 0

    lax.fori_loop(0, nblk, bias_chunk, 0)

    s_bufs, p_bufs, a_bufs = (s0_ref, s1_ref), (p0_ref, p1_ref), (a0_ref, a1_ref)
    _init_softmax_state(acc_ref, m_ref, l_ref, p_bufs, a_bufs)

    def logits(c, slot):
        off = pl.multiple_of(c * ck, ck)
        for pr in range(2):
            s_bufs[slot][:, 2 * pr * tq:2 * (pr + 1) * tq] = _dot_nt(
                k_ref[0, pl.ds(off, ck), pr * LANES:(pr + 1) * LANES], qds[pr])

    def softmax(c, slot, masked):
        del masked
        off = pl.multiple_of(c * ck, ck)
        s = s_bufs[slot][...] + jnp.tile(bias_ref[pl.ds(off, ck), :], (1, 4))
        _softmax_stage(s, m_ref, l_ref, p_bufs[slot], a_bufs[slot])

    def pv(c, slot):
        off = pl.multiple_of(c * ck, ck)
        upd = jnp.concatenate(
            [_dot(vt_ref[h * HEAD_DIM:(h + 1) * HEAD_DIM, pl.ds(off, ck)],
                  p_bufs[slot][:, h * tq:(h + 1) * tq]) for h in range(4)], axis=1)
        acc_ref[...] = a_bufs[slot][...] * acc_ref[...] + upd

    _pipeline_masked_last(nblk - 1, logits, softmax, pv)

    out = acc_ref[...] / l_ref[...]
    out_t = jnp.concatenate([out[:, h * tq:(h + 1) * tq] for h in range(4)], axis=0)
    o_ref[0] = out_t.T.astype(o_ref.dtype)


def _dsa_attention(p3, small3, *, tq=128, ck=512):
    b, s, _ = p3.shape
    ck = min(ck, s)
    topk = min(TOPK_MAX, s // 4)
    return pl.pallas_call(
        functools.partial(_dsa_kernel, tq=tq, ck=ck, topk=topk, seq=s),
        grid=(b, s // tq),
        in_specs=[
            pl.BlockSpec((1, tq, 4 * LANES), lambda bi, qi: (bi, qi, BLK_IQ // 4)),
            pl.BlockSpec((1, tq, 2 * LANES), lambda bi, qi: (bi, qi, BLK_DQ // 2)),
            pl.BlockSpec((1, tq, LANES), lambda bi, qi: (bi, qi, 0)),
            pl.BlockSpec((1, s, LANES), lambda bi, qi: (bi, 0, BLK_IK)),
            pl.BlockSpec((1, s, 2 * LANES), lambda bi, qi: (bi, 0, BLK_DK // 2)),
            pl.BlockSpec((1, s, 2 * LANES), lambda bi, qi: (bi, 0, BLK_DV // 2)),
        ],
        out_specs=pl.BlockSpec((1, tq, 2 * LANES), lambda bi, qi: (bi, qi, 0)),
        out_shape=jax.ShapeDtypeStruct((b, s, 2 * LANES), jnp.bfloat16),
        scratch_shapes=[
            pltpu.VMEM((s, tq), jnp.int32),
            pltpu.VMEM((s, tq), jnp.int16), pltpu.VMEM((s, tq), jnp.int16),
            pltpu.VMEM((s, tq), jnp.float32),
            pltpu.VMEM((4 * HEAD_DIM, s), jnp.bfloat16),
            pltpu.VMEM((1, tq), jnp.int32),
            pltpu.VMEM((HEAD_DIM, 4 * tq), jnp.float32),
            pltpu.VMEM((1, 4 * tq), jnp.float32), pltpu.VMEM((1, 4 * tq), jnp.float32),
            pltpu.VMEM((ck, 4 * tq), jnp.float32), pltpu.VMEM((ck, 4 * tq), jnp.float32),
            pltpu.VMEM((ck, 4 * tq), jnp.bfloat16), pltpu.VMEM((ck, 4 * tq), jnp.bfloat16),
            pltpu.VMEM((1, 4 * tq), jnp.float32), pltpu.VMEM((1, 4 * tq), jnp.float32),
        ],
        compiler_params=_cparams(("arbitrary", "arbitrary")),
        name="dsa_attn",
    )(p3, p3, small3, p3, p3, p3)


def _merge_kernel(x_ref, g_ref, wgate_ref, bgate_ref, ya_ref, yb_ref, yc_ref, yd_ref,
                  wa_ref, wb_ref, wc_ref, wd_ref, wo_ref, o_ref):
    x = x_ref[...]
    d = x.shape[1]
    hb = _rms(x, g_ref[...], NORM_EPS).astype(jnp.bfloat16)
    merged = jnp.zeros(x.shape, jnp.float32)
    for i, (y_ref, w_ref) in enumerate(((ya_ref, wa_ref), (yb_ref, wb_ref), (yc_ref, wc_ref), (yd_ref, wd_ref))):
        gate = jax.nn.sigmoid(_dot(hb, wgate_ref[:, i * d:(i + 1) * d]) + bgate_ref[:, i * d:(i + 1) * d])
        merged = merged + gate * _dot(y_ref[...], w_ref[...])
    o_ref[...] = x + _dot(merged.astype(jnp.bfloat16), wo_ref[...])


def _merge(x2, gain, wgate, bgate, ya, yb, yc, yd, wa, wb, wc, wd, wo, *, tm=512):
    m, d = x2.shape
    rowblk = lambda w: pl.BlockSpec((tm, w), lambda i: (i, 0))
    return pl.pallas_call(
        _merge_kernel,
        grid=(m // tm,),
        in_specs=[
            rowblk(d), _resident((1, d)), _resident((d, N_BRANCH * d)), _resident((1, N_BRANCH * d)),
            rowblk(ya.shape[1]), rowblk(yb.shape[1]), rowblk(yc.shape[1]), rowblk(yd.shape[1]),
            _resident(wa.shape), _resident(wb.shape), _resident(wc.shape), _resident(wd.shape),
            _resident(wo.shape),
        ],
        out_specs=rowblk(d),
        out_shape=jax.ShapeDtypeStruct((m, d), jnp.float32),
        compiler_params=_cparams(("parallel",)),
        name="merge",
    )(x2, gain, wgate, bgate, ya, yb, yc, yd, wa, wb, wc, wd, wo)


def _proj_weights(w_in, b_fgt):
    d = w_in.shape[0]
    o = 0

    def take(width):
        nonlocal o
        seg = w_in[:, o:o + width]
        o += width
        return seg

    aq, ak, av = take(512), take(512), take(512)
    bq, bk, bv, bf = take(256), take(256), take(256), take(4)
    cq, ck, cv = take(256), take(256), take(256)
    dq, dk, dv = take(256), take(256), take(256)
    diq, dik, diw = take(512), take(64), take(8)
    zeros = jnp.zeros((d, LANES), w_in.dtype)
    big = jnp.concatenate([aq, ak, av, bq, bk, bv, cq, ck, cv, dq, dk, dv, dik, dik, zeros, diq], axis=1)
    small = jnp.concatenate([diw, bf, jnp.zeros((d, LANES - 12), w_in.dtype)], axis=1)
    bias = jnp.concatenate([jnp.zeros((8,), jnp.float32), b_fgt.astype(jnp.float32),
                            jnp.zeros((LANES - 12,), jnp.float32)]).reshape(1, LANES)
    return big.astype(jnp.bfloat16), small.astype(jnp.bfloat16), bias


def kernel(x, positions, ffn1_norm, ffn1_w_gu, ffn1_w_down, mix_norm, w_in, b_fgt, lam_q1, lam_k1, lam_q2, lam_k2, diff_gain, w_gate, b_gate, w_br_a, w_br_b, w_br_c, w_br_d, w_out, ffn2_norm, ffn2_w_gu, ffn2_w_down, final_norm):
    b, s, d = x.shape
    depth = w_in.shape[0]
    bf = jnp.bfloat16
    cos_t, sin1_t, sin2_t = _rope_tables(positions)
    fgain = final_norm.reshape(1, d)
    x2 = x.reshape(b * s, d)
    for l in range(depth):
        x2 = _ffn(x2, ffn1_norm[l].reshape(1, d), ffn1_w_gu[l][:, :D_FF].astype(bf),
                  ffn1_w_gu[l][:, D_FF:].astype(bf), ffn1_w_down[l].astype(bf), fgain, final_norm=False)
        wbig, wsmall, bsmall = _proj_weights(w_in[l], b_fgt[l])
        p2, small2 = _proj(x2, mix_norm[l].reshape(1, d), wbig, wsmall, bsmall, cos_t, sin1_t, sin2_t, s)
        p3 = p2.reshape(b, s, PROJ_W)
        small3 = small2.reshape(b, s, LANES)
        cum_n, cum_t = _cumsum(small3)
        lam_rows = jnp.zeros((8, LANES), jnp.float32)
        lam_rows = lam_rows.at[0:4, 0:HEAD_DIM].set(jnp.stack([lam_q1[l], lam_k1[l], lam_q2[l], lam_k2[l]]))
        ya = _diff_attention(p3, lam_rows, diff_gain[l].reshape(1, LANES), l)
        yb = _fox_attention(p3, cum_n, cum_t)
        yc = _sb_attention(p3)
        yd = _dsa_attention(p3, small3)
        x2 = _merge(x2, mix_norm[l].reshape(1, d), w_gate[l].astype(bf), b_gate[l].reshape(1, N_BRANCH * d),
                    ya.reshape(b * s, -1), yb.reshape(b * s, -1), yc.reshape(b * s, -1), yd.reshape(b * s, -1),
                    w_br_a[l].astype(bf), w_br_b[l].astype(bf), w_br_c[l].astype(bf), w_br_d[l].astype(bf),
                    w_out[l].astype(bf))
        x2 = _ffn(x2, ffn2_norm[l].reshape(1, d), ffn2_w_gu[l][:, :D_FF].astype(bf),
                  ffn2_w_gu[l][:, D_FF:].astype(bf), ffn2_w_down[l].astype(bf), fgain,
                  final_norm=(l == depth - 1))
    return x2.reshape(b, s, d)
```

```python
import functools
import math

import numpy as np
import jax
import jax.numpy as jnp
from jax import lax
from jax.experimental import pallas as pl
from jax.experimental.pallas import tpu as pltpu

D_MODEL = 1024
HEAD_DIM = 64
N_IDX_HEADS = 8
TOPK_MAX = 256
ROPE_THETA = 500000.0
ROPE_DIM = HEAD_DIM // 4
D_FF = 2816
N_BRANCH = 4
NORM_EPS = 1e-6
SUBLN_EPS = 1e-5
SCALE = HEAD_DIM ** -0.5
LOG2E = math.log2(math.e)

LANES = 128
VMEM_LIMIT = 56 * 1024 * 1024

BLK_AQ, BLK_AK, BLK_AV = 0, 4, 8
BLK_BQ, BLK_BK, BLK_BV = 12, 14, 16
BLK_CQ, BLK_CK, BLK_CV = 18, 20, 22
BLK_DQ, BLK_DK, BLK_DV = 24, 26, 28
BLK_IK, BLK_SPARE, BLK_IQ = 30, 31, 32
N_PROJ_BLKS = 36
PROJ_W = N_PROJ_BLKS * LANES
ROPE_BLKS = frozenset(list(range(0, 8)) + [24, 25, 26, 27, 30] + list(range(32, 36)))
SM_IW, SM_F = 0, 8

NEG_BIG = -1e30
INT_MIN = -(2 ** 31)


def _cparams(sem, vmem=VMEM_LIMIT):
    return pltpu.CompilerParams(dimension_semantics=sem, vmem_limit_bytes=vmem)


def _resident(shape):
    nd = len(shape)
    return pl.BlockSpec(shape, lambda *_: (0,) * nd, pipeline_mode=pl.Buffered(1))


def _rms(x, g, eps):
    ms = jnp.mean(x * x, axis=-1, keepdims=True)
    return x * lax.rsqrt(ms + eps) * g


def _dot(a, b):
    return jnp.dot(a, b, preferred_element_type=jnp.float32)


def _dot_nt(a, b):
    return lax.dot_general(a, b, (((1,), (1,)), ((), ())), preferred_element_type=jnp.float32)


def _ffn_kernel(x_ref, g_ref, wg_ref, wu_ref, wd_ref, fg_ref, o_ref, *, tf, final_norm):
    x = x_ref[...]
    hb = _rms(x, g_ref[...], NORM_EPS).astype(jnp.bfloat16)
    acc = jnp.zeros(x.shape, jnp.float32)
    for j in range(D_FF // tf):
        sl = slice(j * tf, (j + 1) * tf)
        g = _dot(hb, wg_ref[:, sl])
        u = _dot(hb, wu_ref[:, sl])
        a = (g * jax.nn.sigmoid(g) * u).astype(jnp.bfloat16)
        acc = acc + _dot(a, wd_ref[sl, :])
    y = x + 0.5 * acc
    if final_norm:
        y = _rms(y, fg_ref[...], NORM_EPS)
    o_ref[...] = y


def _ffn(x2, gain, wg, wu, wd, fgain, *, final_norm, tm=512, tf=256):
    m, d = x2.shape
    return pl.pallas_call(
        functools.partial(_ffn_kernel, tf=tf, final_norm=final_norm),
        grid=(m // tm,),
        in_specs=[
            pl.BlockSpec((tm, d), lambda i: (i, 0)),
            _resident((1, d)),
            _resident((d, D_FF)),
            _resident((d, D_FF)),
            _resident((D_FF, d)),
            _resident((1, d)),
        ],
        out_specs=pl.BlockSpec((tm, d), lambda i: (i, 0)),
        out_shape=jax.ShapeDtypeStruct((m, d), jnp.float32),
        compiler_params=_cparams(("parallel",)),
        name="ffn",
    )(x2, gain, wg, wu, wd, fgain)


def _rope_kernel(pos_ref, freq_ref, m1_ref, m2_ref, c_ref, s1_ref, s2_ref):
    ang = pos_ref[...].astype(jnp.float32) * freq_ref[...]
    c = jnp.cos(ang)
    s = jnp.sin(ang)
    c_ref[...] = c
    s1_ref[...] = s * m1_ref[...]
    s2_ref[...] = s * m2_ref[...]


def _rope_tables(positions):
    s = positions.shape[0]
    freqs = ROPE_THETA ** (-jnp.arange(0, ROPE_DIM, 2, dtype=jnp.float32) / ROPE_DIM)
    half = ROPE_DIM // 2
    d = np.arange(LANES) % HEAD_DIM
    freq_row = jnp.where(d < ROPE_DIM, freqs[d % half], 0.0).reshape(1, LANES)
    m1 = jnp.asarray(np.where(d < half, -1.0, 0.0).astype(np.float32)).reshape(1, LANES)
    m2 = jnp.asarray(np.where((d >= half) & (d < ROPE_DIM), 1.0, 0.0).astype(np.float32)).reshape(1, LANES)
    tab = jax.ShapeDtypeStruct((s, LANES), jnp.float32)
    ts = min(s, 1024)
    row = pl.BlockSpec((1, LANES), lambda i: (0, 0))
    blk = pl.BlockSpec((ts, LANES), lambda i: (i, 0))
    return pl.pallas_call(
        _rope_kernel,
        grid=(s // ts,),
        in_specs=[pl.BlockSpec((ts, 1), lambda i: (i, 0)), row, row, row],
        out_specs=[blk, blk, blk],
        out_shape=[tab, tab, tab],
        compiler_params=_cparams(("parallel",)),
        name="rope_tables",
    )(positions.reshape(s, 1), freq_row, m1, m2)


def _proj_kernel(x_ref, g_ref, w_ref, wsm_ref, bsm_ref, c_ref, s1_ref, s2_ref, p_ref, sm_ref):
    hb = _rms(x_ref[...], g_ref[...], NORM_EPS).astype(jnp.bfloat16)
    c, s1, s2 = c_ref[...], s1_ref[...], s2_ref[...]
    chunk = 4
    for cb in range(N_PROJ_BLKS // chunk):
        y = _dot(hb, w_ref[:, cb * chunk * LANES:(cb + 1) * chunk * LANES])
        for sub in range(chunk):
            blk = cb * chunk + sub
            ys = y[:, sub * LANES:(sub + 1) * LANES]
            if blk in ROPE_BLKS:
                ys = (ys * c + pltpu.roll(ys, LANES - ROPE_DIM // 2, 1) * s1
                      + pltpu.roll(ys, ROPE_DIM // 2, 1) * s2)
            p_ref[:, blk * LANES:(blk + 1) * LANES] = ys.astype(jnp.bfloat16)
    sm_ref[...] = _dot(hb, wsm_ref[...]) + bsm_ref[...]


def _proj(x2, gain, w, wsm, bsm, c, s1, s2, seq, *, tm=512):
    m, d = x2.shape
    tm = min(tm, seq)
    nrep = seq // tm
    tab = pl.BlockSpec((tm, LANES), lambda i: (i % nrep, 0))
    return pl.pallas_call(
        _proj_kernel,
        grid=(m // tm,),
        in_specs=[
            pl.BlockSpec((tm, d), lambda i: (i, 0)),
            _resident((1, d)),
            _resident((d, PROJ_W)),
            _resident((d, LANES)),
            _resident((1, LANES)),
            tab, tab, tab,
        ],
        out_specs=[pl.BlockSpec((tm, PROJ_W), lambda i: (i, 0)),
                   pl.BlockSpec((tm, LANES), lambda i: (i, 0))],
        out_shape=[jax.ShapeDtypeStruct((m, PROJ_W), jnp.bfloat16),
                   jax.ShapeDtypeStruct((m, LANES), jnp.float32)],
        compiler_params=_cparams(("parallel",)),
        name="proj",
    )(x2, gain, w, wsm, bsm, c, s1, s2)


def _cumsum_kernel(sm_ref, cn_ref, ct_ref):
    t = sm_ref[0].T
    n = t.shape[1]
    ls = jnp.minimum(t, 0.0) - jnp.log(1.0 + jnp.exp(-jnp.abs(t)))
    lane = lax.broadcasted_iota(jnp.int32, ls.shape, 1)
    sh = 1
    while sh < n:
        ls = ls + jnp.where(lane >= sh, pltpu.roll(ls, sh, 1), 0.0)
        sh *= 2
    ls = ls * LOG2E
    ct_ref[0] = ls[SM_F:SM_F + 8, :]
    for h in range(4):
        cn_ref[0, h] = jnp.broadcast_to(ls[SM_F + h:SM_F + h + 1, :], ls.shape).T


def _cumsum(small3):
    b, s, _ = small3.shape
    return pl.pallas_call(
        _cumsum_kernel,
        grid=(b,),
        in_specs=[pl.BlockSpec((1, s, LANES), lambda i: (i, 0, 0))],
        out_specs=[pl.BlockSpec((1, 4, s, LANES), lambda i: (i, 0, 0, 0)),
                   pl.BlockSpec((1, 8, s), lambda i: (i, 0, 0))],
        out_shape=[jax.ShapeDtypeStruct((b, 4, s, LANES), jnp.float32),
                   jax.ShapeDtypeStruct((b, 8, s), jnp.float32)],
        compiler_params=_cparams(("parallel",)),
        name="forget_cumsum",
    )(small3)


def _half_masks(dtype):
    lane = lax.broadcasted_iota(jnp.int32, (1, LANES), 1)
    lo = (lane < HEAD_DIM).astype(dtype)
    return lo, (1 - lo).astype(dtype)


def _causal_mask(tq, tk, strict):
    r = lax.broadcasted_iota(jnp.int32, (tq, tk), 0)
    c = lax.broadcasted_iota(jnp.int32, (tq, tk), 1)
    return (c < r) if strict else (c <= r)


def _softmax2_step(s, m, l):
    m_new = jnp.maximum(m, jnp.max(s, axis=-1, keepdims=True))
    alpha = jnp.exp2(m - m_new)
    p = jnp.exp2(s - m_new)
    l_new = alpha * l + jnp.sum(p, axis=-1, keepdims=True)
    return m_new, l_new, alpha, p


def _softmax_step(s, m, l):
    m_new = jnp.maximum(m, jnp.max(s, axis=-1, keepdims=True))
    alpha = jnp.exp(m - m_new)
    p = jnp.exp(s - m_new)
    l_new = alpha * l + jnp.sum(p, axis=-1, keepdims=True)
    return m_new, l_new, alpha, p


def _pipeline_masked_last(n, logits, softmax, pv):
    logits(0, 0)

    def body(i, _):
        s = 2 * i
        logits(s + 1, 1)
        softmax(s, 0, False)
        pv(jnp.maximum(s - 1, 0), 1)
        logits(s + 2, 0)
        softmax(s + 1, 1, False)
        pv(s, 0)
        return 0

    lax.fori_loop(0, n // 2, body, 0)

    @pl.when(n % 2 == 0)
    def _():
        pv(jnp.maximum(n - 1, 0), 1)
        softmax(n, 0, True)
        pv(n, 0)

    @pl.when(n % 2 == 1)
    def _():
        logits(n, 1)
        softmax(n - 1, 0, False)
        pv(jnp.maximum(n - 2, 0), 1)
        softmax(n, 1, True)
        pv(n - 1, 0)
        pv(n, 1)


def _pipeline_pair(n_steps, logits, weights, pv):
    assert n_steps % 2 == 1 and n_steps >= 3
    logits(0, 0)
    logits(1, 1)
    weights(0, 0, True)
    logits(2, 0)
    weights(1, 1, True)
    pv(0, 0)

    def body(i, _):
        s = 2 * i
        logits(s + 1, 1)
        weights(s, 0, False)
        pv(s - 1, 1)
        logits(s + 2, 0)
        weights(s + 1, 1, False)
        pv(s, 0)
        return 0

    lax.fori_loop(1, n_steps // 2, body, 0)
    pv(n_steps - 2, 1)
    weights(n_steps - 1, 0, False)
    pv(n_steps - 1, 0)


def _pair_steps(qi, nq, reverse=False):
    def where(step):
        u = step - 2
        in_b = jnp.where(step < 2, step == 1, u >= qi)
        v = jnp.where(in_b, u - qi, u)
        n_other = jnp.where(in_b, nq - 1 - qi, qi)
        blk = (n_other - 1 - v) if reverse else v
        return in_b.astype(jnp.int32), jnp.where(step < 2, n_other, blk)

    return where


def _build_vt(v_ref, vt_ref, seq, chunk=512):
    chunk = min(chunk, seq)
    for c in range(seq // chunk):
        blk = v_ref[0, c * chunk:(c + 1) * chunk, :].astype(jnp.float32)
        vt_ref[:, c * chunk:(c + 1) * chunk] = blk.T.astype(vt_ref.dtype)


def _softmax_stage(s, m_ref, l_ref, p_buf, a_buf):
    m = m_ref[...]
    m_new = jnp.maximum(m, jnp.max(s, axis=0, keepdims=True))
    alpha = jnp.exp2(m - m_new)
    p = jnp.exp2(s - m_new)
    l_ref[...] = alpha * l_ref[...] + jnp.sum(p, axis=0, keepdims=True)
    m_ref[...] = m_new
    a_buf[...] = alpha
    p_buf[...] = p.astype(jnp.bfloat16)


def _init_pair_state(acc_ref, m_ref, l_ref):
    acc_ref[...] = jnp.zeros_like(acc_ref)
    m_ref[...] = jnp.full(m_ref.shape, NEG_BIG, jnp.float32)
    l_ref[...] = jnp.zeros_like(l_ref)


def _init_softmax_state(acc_ref, m_ref, l_ref, p_bufs, a_bufs):
    acc_ref[...] = jnp.zeros_like(acc_ref)
    m_ref[...] = jnp.full(m_ref.shape, NEG_BIG, jnp.float32)
    l_ref[...] = jnp.zeros_like(l_ref)
    p_bufs[1][...] = jnp.zeros_like(p_bufs[1])
    a_bufs[1][...] = jnp.ones_like(a_bufs[1])


def _diff_kernel(qa_ref, qb_ref, k_ref, v_ref, lam_ref, gain_ref, oa_ref, ob_ref,
                 vt_ref, qs_ref, acc_ref, m_ref, l_ref,
                 s0_ref, s1_ref, p0_ref, p1_ref, a0_ref, a1_ref, *, t, lam_init, seq):
    qi = pl.program_id(2)
    nq = seq // t
    s_bufs, p_bufs, a_bufs = (s0_ref, s1_ref), (p0_ref, p1_ref), (a0_ref, a1_ref)

    @pl.when(qi == 0)
    def _():
        _build_vt(v_ref, vt_ref, seq)

    qs_ref[0] = _stack_heads(qa_ref[0], SCALE * LOG2E)
    qs_ref[1] = _stack_heads(qb_ref[0], SCALE * LOG2E)
    _init_pair_state(acc_ref, m_ref, l_ref)
    where = _pair_steps(qi, nq)

    def logits(step, slot):
        tile, blk = where(step)
        s_bufs[slot][...] = _dot_nt(k_ref[0, pl.ds(pl.multiple_of(blk * t, t), t), :], qs_ref[tile])

    def softmax(step, slot, masked):
        tile, _ = where(step)
        s = s_bufs[slot][...]
        if masked:
            s = jnp.where(_stacked_causal(t, False), s, -jnp.inf)
        _softmax_stage(s, m_ref.at[tile], l_ref.at[tile], p_bufs[slot], a_bufs[slot])

    def pv(step, slot):
        tile, blk = where(step)
        vt = vt_ref[:, pl.ds(pl.multiple_of(blk * t, t), t)]
        acc_ref[tile] = a_bufs[slot][...] * acc_ref[tile] + _dot(vt, p_bufs[slot][...])

    _pipeline_pair(nq + 1, logits, softmax, pv)

    lp = lam_ref[...]
    lam = (jnp.exp(jnp.sum(lp[0:1] * lp[1:2], axis=-1, keepdims=True))
           - jnp.exp(jnp.sum(lp[2:3] * lp[3:4], axis=-1, keepdims=True)) + lam_init)
    for tile, o_ref in enumerate((oa_ref, ob_ref)):
        yt = acc_ref[tile] / l_ref[tile]
        y = (yt[:, :t] - lam * yt[:, t:]).T
        y = _rms(y, gain_ref[...], SUBLN_EPS) * (1.0 - lam_init)
        o_ref[0] = y.astype(o_ref.dtype)


def _pair_out(b, s, width):
    return [jax.ShapeDtypeStruct((b, s // 2, width), jnp.bfloat16)] * 2


def _join_halves(lo, hi):
    return jnp.concatenate([lo, hi], axis=1)


def _diff_attention(p3, lam_rows, gain, layer_idx, *, t=256):
    b, s, _ = p3.shape
    t = min(t, s // 2)
    nq = s // t
    lam_init = 0.8 - 0.6 * math.exp(-0.3 * layer_idx)
    nh = 4
    return _join_halves(*pl.pallas_call(
        functools.partial(_diff_kernel, t=t, lam_init=lam_init, seq=s),
        grid=(b, nh, nq // 2),
        in_specs=[
            pl.BlockSpec((1, t, LANES), lambda bi, h, qi: (bi, qi, BLK_AQ + h)),
            pl.BlockSpec((1, t, LANES), lambda bi, h, qi: (bi, nq - 1 - qi, BLK_AQ + h)),
            pl.BlockSpec((1, s, LANES), lambda bi, h, qi: (bi, 0, BLK_AK + h)),
            pl.BlockSpec((1, s, LANES), lambda bi, h, qi: (bi, 0, BLK_AV + h)),
            pl.BlockSpec((8, LANES), lambda bi, h, qi: (0, 0)),
            pl.BlockSpec((1, LANES), lambda bi, h, qi: (0, 0)),
        ],
        out_specs=[pl.BlockSpec((1, t, LANES), lambda bi, h, qi: (bi, qi, h)),
                   pl.BlockSpec((1, t, LANES), lambda bi, h, qi: (bi, nq // 2 - 1 - qi, h))],
        out_shape=_pair_out(b, s, nh * LANES),
        scratch_shapes=_flash_t_scratch(s, t),
        compiler_params=_cparams(("arbitrary", "arbitrary", "arbitrary")),
        name="diff_attn",
    )(p3, p3, p3, p3, lam_rows, gain))


def _stack_heads(q, scale):
    lo, hi = _half_masks(jnp.float32)
    q = q.astype(jnp.float32) * scale
    return jnp.concatenate([(q * lo).astype(jnp.bfloat16), (q * hi).astype(jnp.bfloat16)], axis=0)


def _stacked_causal(t, strict):
    kr = lax.broadcasted_iota(jnp.int32, (t, 2 * t), 0)
    qc = lax.broadcasted_iota(jnp.int32, (t, 2 * t), 1)
    qc = jnp.where(qc >= t, qc - t, qc)
    return (kr < qc) if strict else (kr <= qc)


def _unstack_heads_t(acc, t):
    return jnp.concatenate([acc[:HEAD_DIM, :t], acc[HEAD_DIM:, t:]], axis=0).T


def _fox_kernel(qa_ref, qb_ref, k_ref, v_ref, cr_ref, ct_ref, oa_ref, ob_ref,
                vt_ref, qs_ref, acc_ref, m_ref, l_ref,
                s0_ref, s1_ref, p0_ref, p1_ref, a0_ref, a1_ref, cq_ref, *, t, seq):
    pair = pl.program_id(1)
    qi = pl.program_id(2)
    nq = seq // t
    s_bufs, p_bufs, a_bufs = (s0_ref, s1_ref), (p0_ref, p1_ref), (a0_ref, a1_ref)

    @pl.when(qi == 0)
    def _():
        _build_vt(v_ref, vt_ref, seq)

    for tile, (q_ref, row) in enumerate(((qa_ref, qi), (qb_ref, nq - 1 - qi))):
        qs_ref[tile] = _stack_heads(q_ref[0], SCALE * LOG2E)
        qoff = pl.multiple_of(row * t, t)
        cq_ref[tile] = jnp.concatenate(
            [ct_ref[0, pl.ds(2 * pair + i, 1), pl.ds(qoff, t)] for i in range(2)], axis=1)
    _init_pair_state(acc_ref, m_ref, l_ref)
    where = _pair_steps(qi, nq)

    def logits(step, slot):
        tile, blk = where(step)
        s_bufs[slot][...] = _dot_nt(k_ref[0, pl.ds(pl.multiple_of(blk * t, t), t), :], qs_ref[tile])

    def softmax(step, slot, masked):
        tile, blk = where(step)
        off = pl.multiple_of(blk * t, t)
        ck = jnp.concatenate([jnp.tile(cr_ref[0, i, pl.ds(off, t), :], (1, t // LANES)) for i in range(2)],
                             axis=1)
        s = (s_bufs[slot][...] + cq_ref[tile]) - ck
        if masked:
            s = jnp.where(_stacked_causal(t, False), s, -jnp.inf)
        _softmax_stage(s, m_ref.at[tile], l_ref.at[tile], p_bufs[slot], a_bufs[slot])

    def pv(step, slot):
        tile, blk = where(step)
        vt = vt_ref[:, pl.ds(pl.multiple_of(blk * t, t), t)]
        acc_ref[tile] = a_bufs[slot][...] * acc_ref[tile] + _dot(vt, p_bufs[slot][...])

    _pipeline_pair(nq + 1, logits, softmax, pv)

    for tile, o_ref in enumerate((oa_ref, ob_ref)):
        o_ref[0] = _unstack_heads_t(acc_ref[tile] / l_ref[tile], t).astype(o_ref.dtype)


def _flash_t_scratch(s, t):
    row = pltpu.VMEM((1, 2 * t), jnp.float32)
    rows = pltpu.VMEM((2, 1, 2 * t), jnp.float32)
    return [pltpu.VMEM((LANES, s), jnp.bfloat16),
            pltpu.VMEM((2, 2 * t, LANES), jnp.bfloat16),
            pltpu.VMEM((2, LANES, 2 * t), jnp.float32),
            rows, rows,
            pltpu.VMEM((t, 2 * t), jnp.float32), pltpu.VMEM((t, 2 * t), jnp.float32),
            pltpu.VMEM((t, 2 * t), jnp.bfloat16), pltpu.VMEM((t, 2 * t), jnp.bfloat16),
            row, row]


def _fox_attention(p3, cum_rep, cum_t, *, t=256):
    b, s, _ = p3.shape
    t = min(t, s // 2)
    nq = s // t
    return _join_halves(*pl.pallas_call(
        functools.partial(_fox_kernel, t=t, seq=s),
        grid=(b, 2, nq // 2),
        in_specs=[
            pl.BlockSpec((1, t, LANES), lambda bi, p, qi: (bi, qi, BLK_BQ + p)),
            pl.BlockSpec((1, t, LANES), lambda bi, p, qi: (bi, nq - 1 - qi, BLK_BQ + p)),
            pl.BlockSpec((1, s, LANES), lambda bi, p, qi: (bi, 0, BLK_BK + p)),
            pl.BlockSpec((1, s, LANES), lambda bi, p, qi: (bi, 0, BLK_BV + p)),
            pl.BlockSpec((1, 2, s, LANES), lambda bi, p, qi: (bi, p, 0, 0)),
            pl.BlockSpec((1, 8, s), lambda bi, p, qi: (bi, 0, 0)),
        ],
        out_specs=[pl.BlockSpec((1, t, LANES), lambda bi, p, qi: (bi, qi, p)),
                   pl.BlockSpec((1, t, LANES), lambda bi, p, qi: (bi, nq // 2 - 1 - qi, p))],
        out_shape=_pair_out(b, s, 2 * LANES),
        scratch_shapes=_flash_t_scratch(s, t) + [pltpu.VMEM((2, 1, 2 * t), jnp.float32)],
        compiler_params=_cparams(("arbitrary", "arbitrary", "arbitrary")),
        name="fox_attn",
    )(p3, p3, p3, p3, cum_rep, cum_t))


def _sb_kernel(qa_ref, qb_ref, k_ref, v_ref, oa_ref, ob_ref,
               vt_ref, qs_ref, acc_ref, run_ref,
               s0_ref, s1_ref, p0_ref, p1_ref, *, t, seq):
    qi = pl.program_id(2)
    nq = seq // t
    s_bufs, p_bufs = (s0_ref, s1_ref), (p0_ref, p1_ref)

    @pl.when(qi == 0)
    def _():
        _build_vt(v_ref, vt_ref, seq)

    qs_ref[0] = _stack_heads(qa_ref[0], SCALE * LOG2E)
    qs_ref[1] = _stack_heads(qb_ref[0], SCALE * LOG2E)
    r = lax.broadcasted_iota(jnp.int32, (t, 2 * t), 0)
    c = lax.broadcasted_iota(jnp.int32, (t, 2 * t), 1)
    later2 = (jnp.where(c >= t, c - t, c) > r).astype(jnp.bfloat16)
    acc_ref[...] = jnp.zeros_like(acc_ref)
    run_ref[...] = jnp.zeros_like(run_ref)
    where = _pair_steps(qi, nq, reverse=True)

    def logits(step, slot):
        tile, blk = where(step)
        s_bufs[slot][...] = _dot_nt(k_ref[0, pl.ds(pl.multiple_of(blk * t, t), t), :], qs_ref[tile])

    def weights(step, slot, masked):
        tile, _ = where(step)
        z = s_bufs[slot][...]
        log_beta = jnp.minimum(z, 0.0) - jnp.log2(1.0 + jnp.exp2(-jnp.abs(z)))
        l1m = log_beta - z
        if masked:
            cm = _stacked_causal(t, True)
            l1m = jnp.where(cm, l1m, 0.0)
        l_hi = l1m.astype(jnp.bfloat16)
        l_lo = (l1m - l_hi.astype(jnp.float32)).astype(jnp.bfloat16)
        after = _dot(later2, jnp.concatenate([l_hi, l_lo], axis=0)) + run_ref[tile]
        a = jnp.exp2(log_beta + after)
        if masked:
            a = jnp.where(cm, a, 0.0)
        p_bufs[slot][...] = a.astype(jnp.bfloat16)
        run_ref[tile] += jnp.sum(l1m, axis=0, keepdims=True)

    def pv(step, slot):
        tile, blk = where(step)
        acc_ref[tile] += _dot(vt_ref[:, pl.ds(pl.multiple_of(blk * t, t), t)], p_bufs[slot][...])

    _pipeline_pair(nq + 1, logits, weights, pv)

    for tile, o_ref in enumerate((oa_ref, ob_ref)):
        o_ref[0] = _unstack_heads_t(acc_ref[tile], t).astype(o_ref.dtype)


def _sb_attention(p3, *, t=256):
    b, s, _ = p3.shape
    t = min(t, s // 2)
    nq = s // t
    return _join_halves(*pl.pallas_call(
        functools.partial(_sb_kernel, t=t, seq=s),
        grid=(b, 2, nq // 2),
        in_specs=[
            pl.BlockSpec((1, t, LANES), lambda bi, p, qi: (bi, qi, BLK_CQ + p)),
            pl.BlockSpec((1, t, LANES), lambda bi, p, qi: (bi, nq - 1 - qi, BLK_CQ + p)),
            pl.BlockSpec((1, s, LANES), lambda bi, p, qi: (bi, 0, BLK_CK + p)),
            pl.BlockSpec((1, s, LANES), lambda bi, p, qi: (bi, 0, BLK_CV + p)),
        ],
        out_specs=[pl.BlockSpec((1, t, LANES), lambda bi, p, qi: (bi, qi, p)),
                   pl.BlockSpec((1, t, LANES), lambda bi, p, qi: (bi, nq // 2 - 1 - qi, p))],
        out_shape=_pair_out(b, s, 2 * LANES),
        scratch_shapes=[pltpu.VMEM((LANES, s), jnp.bfloat16),
                        pltpu.VMEM((2, 2 * t, LANES), jnp.bfloat16),
                        pltpu.VMEM((2, LANES, 2 * t), jnp.float32),
                        pltpu.VMEM((2, 1, 2 * t), jnp.float32),
                        pltpu.VMEM((t, 2 * t), jnp.float32), pltpu.VMEM((t, 2 * t), jnp.float32),
                        pltpu.VMEM((t, 2 * t), jnp.bfloat16), pltpu.VMEM((t, 2 * t), jnp.bfloat16)],
        compiler_params=_cparams(("arbitrary", "arbitrary", "arbitrary")),
        name="sb_attn",
    )(p3, p3, p3, p3))


def _dsa_kernel(iq_ref, q_ref, sm_ref, ik_ref, k_ref, v_ref, o_ref,
                keys_ref, bias_ref, vt_ref, sig_ref, acc_ref, m_ref, l_ref,
                s0_ref, s1_ref, p0_ref, p1_ref, a0_ref, a1_ref,
                *, tq, ck, topk, seq):
    qi = pl.program_id(1)
    nblk = (qi * tq + tq + ck - 1) // ck
    lo, hi = _half_masks(jnp.float32)
    halves = (lo, hi)

    @pl.when(qi == 0)
    def _():
        _build_vt(v_ref, vt_ref, seq)

    iw_t = sm_ref[0].T[SM_IW:SM_IW + N_IDX_HEADS, :]
    iq = iq_ref[0].astype(jnp.float32)
    iq_stack = jnp.concatenate(
        [(iq[:, (h // 2) * LANES:(h // 2 + 1) * LANES] * halves[h % 2]).astype(jnp.bfloat16)
         for h in range(N_IDX_HEADS)], axis=0)
    qds = [_stack_heads(q_ref[0][:, pr * LANES:(pr + 1) * LANES], SCALE * LOG2E) for pr in range(2)]
    t_idx = qi * tq + lax.broadcasted_iota(jnp.int32, (1, tq), 1)
    row = lax.broadcasted_iota(jnp.int32, (ck, tq), 0)

    def score_chunk(c, _):
        off = pl.multiple_of(c * ck, ck)
        ikc = ik_ref[0, pl.ds(off, ck), :]
        rel = _dot_nt(ikc, iq_stack)
        acc = jnp.zeros((ck, tq), jnp.float32)
        for h in range(N_IDX_HEADS):
            acc = acc + iw_t[h:h + 1, :] * jnp.maximum(rel[:, h * tq:(h + 1) * tq], 0.0)
        bits = pltpu.bitcast(acc, jnp.int32)
        key = bits ^ ((bits >> 31) & 0x7FFFFFFF)
        keys_ref[pl.ds(off, ck), :] = jnp.where(row + off <= t_idx, key, INT_MIN)
        return 0

    lax.fori_loop(0, nblk, score_chunk, 0)

    def count(pred_fn):
        def body(c, cnt):
            off = pl.multiple_of(c * ck, ck)
            hit = pred_fn(keys_ref[pl.ds(off, ck), :], off).astype(jnp.int32)
            return cnt + jnp.sum(hit.reshape(ck // 8, 8, tq), axis=0)
        cnt8 = lax.fori_loop(0, nblk, body, jnp.zeros((8, tq), jnp.int32))
        return jnp.sum(cnt8, axis=0, keepdims=True)

    c0 = count(lambda kk, off: kk >= 0)
    tau0 = jnp.where(c0 >= topk, 0, INT_MIN).astype(jnp.int32)

    def bit_step(i, tau):
        cand = tau + jnp.left_shift(jnp.int32(1), 30 - i)
        cnt = count(lambda kk, off: kk >= cand)
        return jnp.where(cnt >= topk, cand, tau)

    tau = lax.fori_loop(0, 31, bit_step, tau0)
    cnt_ge = count(lambda kk, off: kk >= tau)
    real = tau != INT_MIN
    sig_ref[...] = jnp.where(real, seq, -1).astype(jnp.int32)
    over = jnp.max(jnp.where(real & (cnt_ge > topk), 1, 0))

    @pl.when(over > 0)
    def _():
        cnt_gt = count(lambda kk, off: kk > tau)
        need = topk - cnt_gt

        def idx_step(i, x):
            cand = x + jnp.left_shift(jnp.int32(1), (seq.bit_length() - 2) - i)
            f = count(lambda kk, off: (kk == tau) & (row + off < cand))
            return jnp.where(f < need, cand, x)

        x = lax.fori_loop(0, seq.bit_length() - 1, idx_step, jnp.zeros((1, tq), jnp.int32))
        sig_ref[...] = jnp.where(real, x, -1)

    sigma = sig_ref[...]

    def bias_chunk(c, _):
        off = pl.multiple_of(c * ck, ck)
        kk = keys_ref[pl.ds(off, ck), :]
        sel = (kk > tau) | ((kk == tau) & (row + off <= sigma))
        bias_ref[pl.ds(off, ck), :] = jnp.where(sel, 0.0, -jnp.inf)
        return 0

    lax.fori_loop(0, nblk, bias_chunk, 0)

    s_bufs, p_bufs, a_bufs = (s0_ref, s1_ref), (p0_ref, p1_ref), (a0_ref, a1_ref)
    _init_softmax_state(acc_ref, m_ref, l_ref, p_bufs, a_bufs)

    def logits(c, slot):
        off = pl.multiple_of(c * ck, ck)
        for pr in range(2):
            s_bufs[slot][:, 2 * pr * tq:2 * (pr + 1) * tq] = _dot_nt(
                k_ref[0, pl.ds(off, ck), pr * LANES:(pr + 1) * LANES], qds[pr])

    def softmax(c, slot, masked):
        del masked
        off = pl.multiple_of(c * ck, ck)
        s = s_bufs[slot][...] + jnp.tile(bias_ref[pl.ds(off, ck), :], (1, 4))
        _softmax_stage(s, m_ref, l_ref, p_bufs[slot], a_bufs[slot])

    def pv(c, slot):
        off = pl.multiple_of(c * ck, ck)
        upd = jnp.concatenate(
            [_dot(vt_ref[h * HEAD_DIM:(h + 1) * HEAD_DIM, pl.ds(off, ck)],
                  p_bufs[slot][:, h * tq:(h + 1) * tq]) for h in range(4)], axis=1)
        acc_ref[...] = a_bufs[slot][...] * acc_ref[...] + upd

    _pipeline_masked_last(nblk - 1, logits, softmax, pv)

    out = acc_ref[...] / l_ref[...]
    out_t = jnp.concatenate([out[:, h * tq:(h + 1) * tq] for h in range(4)], axis=0)
    o_ref[0] = out_t.T.astype(o_ref.dtype)


def _dsa_attention(p3, small3, *, tq=128, ck=512):
    b, s, _ = p3.shape
    ck = min(ck, s)
    topk = min(TOPK_MAX, s // 4)
    return pl.pallas_call(
        functools.partial(_dsa_kernel, tq=tq, ck=ck, topk=topk, seq=s),
        grid=(b, s // tq),
        in_specs=[
            pl.BlockSpec((1, tq, 4 * LANES), lambda bi, qi: (bi, qi, BLK_IQ // 4)),
            pl.BlockSpec((1, tq, 2 * LANES), lambda bi, qi: (bi, qi, BLK_DQ // 2)),
            pl.BlockSpec((1, tq, LANES), lambda bi, qi: (bi, qi, 0)),
            pl.BlockSpec((1, s, LANES), lambda bi, qi: (bi, 0, BLK_IK)),
            pl.BlockSpec((1, s, 2 * LANES), lambda bi, qi: (bi, 0, BLK_DK // 2)),
            pl.BlockSpec((1, s, 2 * LANES), lambda bi, qi: (bi, 0, BLK_DV // 2)),
        ],
        out_specs=pl.BlockSpec((1, tq, 2 * LANES), lambda bi, qi: (bi, qi, 0)),
        out_shape=jax.ShapeDtypeStruct((b, s, 2 * LANES), jnp.bfloat16),
        scratch_shapes=[
            pltpu.VMEM((s, tq), jnp.int32),
            pltpu.VMEM((s, tq), jnp.float32),
            pltpu.VMEM((4 * HEAD_DIM, s), jnp.bfloat16),
            pltpu.VMEM((1, tq), jnp.int32),
            pltpu.VMEM((HEAD_DIM, 4 * tq), jnp.float32),
            pltpu.VMEM((1, 4 * tq), jnp.float32), pltpu.VMEM((1, 4 * tq), jnp.float32),
            pltpu.VMEM((ck, 4 * tq), jnp.float32), pltpu.VMEM((ck, 4 * tq), jnp.float32),
            pltpu.VMEM((ck, 4 * tq), jnp.bfloat16), pltpu.VMEM((ck, 4 * tq), jnp.bfloat16),
            pltpu.VMEM((1, 4 * tq), jnp.float32), pltpu.VMEM((1, 4 * tq), jnp.float32),
        ],
        compiler_params=_cparams(("arbitrary", "arbitrary")),
        name="dsa_attn",
    )(p3, p3, small3, p3, p3, p3)


def _merge_kernel(x_ref, g_ref, wgate_ref, bgate_ref, ya_ref, yb_ref, yc_ref, yd_ref,
                  wa_ref, wb_ref, wc_ref, wd_ref, wo_ref, o_ref):
    x = x_ref[...]
    d = x.shape[1]
    hb = _rms(x, g_ref[...], NORM_EPS).astype(jnp.bfloat16)
    merged = jnp.zeros(x.shape, jnp.float32)
    for i, (y_ref, w_ref) in enumerate(((ya_ref, wa_ref), (yb_ref, wb_ref), (yc_ref, wc_ref), (yd_ref, wd_ref))):
        gate = jax.nn.sigmoid(_dot(hb, wgate_ref[:, i * d:(i + 1) * d]) + bgate_ref[:, i * d:(i + 1) * d])
        merged = merged + gate * _dot(y_ref[...], w_ref[...])
    o_ref[...] = x + _dot(merged.astype(jnp.bfloat16), wo_ref[...])


def _merge(x2, gain, wgate, bgate, ya, yb, yc, yd, wa, wb, wc, wd, wo, *, tm=512):
    m, d = x2.shape
    rowblk = lambda w: pl.BlockSpec((tm, w), lambda i: (i, 0))
    return pl.pallas_call(
        _merge_kernel,
        grid=(m // tm,),
        in_specs=[
            rowblk(d), _resident((1, d)), _resident((d, N_BRANCH * d)), _resident((1, N_BRANCH * d)),
            rowblk(ya.shape[1]), rowblk(yb.shape[1]), rowblk(yc.shape[1]), rowblk(yd.shape[1]),
            _resident(wa.shape), _resident(wb.shape), _resident(wc.shape), _resident(wd.shape),
            _resident(wo.shape),
        ],
        out_specs=rowblk(d),
        out_shape=jax.ShapeDtypeStruct((m, d), jnp.float32),
        compiler_params=_cparams(("parallel",)),
        name="merge",
    )(x2, gain, wgate, bgate, ya, yb, yc, yd, wa, wb, wc, wd, wo)


def _proj_weights(w_in, b_fgt):
    d = w_in.shape[0]
    o = 0

    def take(width):
        nonlocal o
        seg = w_in[:, o:o + width]
        o += width
        return seg

    aq, ak, av = take(512), take(512), take(512)
    bq, bk, bv, bf = take(256), take(256), take(256), take(4)
    cq, ck, cv = take(256), take(256), take(256)
    dq, dk, dv = take(256), take(256), take(256)
    diq, dik, diw = take(512), take(64), take(8)
    zeros = jnp.zeros((d, LANES), w_in.dtype)
    big = jnp.concatenate([aq, ak, av, bq, bk, bv, cq, ck, cv, dq, dk, dv, dik, dik, zeros, diq], axis=1)
    small = jnp.concatenate([diw, bf, jnp.zeros((d, LANES - 12), w_in.dtype)], axis=1)
    bias = jnp.concatenate([jnp.zeros((8,), jnp.float32), b_fgt.astype(jnp.float32),
                            jnp.zeros((LANES - 12,), jnp.float32)]).reshape(1, LANES)
    return big.astype(jnp.bfloat16), small.astype(jnp.bfloat16), bias


def kernel(x, positions, ffn1_norm, ffn1_w_gu, ffn1_w_down, mix_norm, w_in, b_fgt, lam_q1, lam_k1, lam_q2, lam_k2, diff_gain, w_gate, b_gate, w_br_a, w_br_b, w_br_c, w_br_d, w_out, ffn2_norm, ffn2_w_gu, ffn2_w_down, final_norm):
    b, s, d = x.shape
    depth = w_in.shape[0]
    bf = jnp.bfloat16
    cos_t, sin1_t, sin2_t = _rope_tables(positions)
    fgain = final_norm.reshape(1, d)
    x2 = x.reshape(b * s, d)
    for l in range(depth):
        x2 = _ffn(x2, ffn1_norm[l].reshape(1, d), ffn1_w_gu[l][:, :D_FF].astype(bf),
                  ffn1_w_gu[l][:, D_FF:].astype(bf), ffn1_w_down[l].astype(bf), fgain, final_norm=False)
        wbig, wsmall, bsmall = _proj_weights(w_in[l], b_fgt[l])
        p2, small2 = _proj(x2, mix_norm[l].reshape(1, d), wbig, wsmall, bsmall, cos_t, sin1_t, sin2_t, s)
        p3 = p2.reshape(b, s, PROJ_W)
        small3 = small2.reshape(b, s, LANES)
        cum_n, cum_t = _cumsum(small3)
        lam_rows = jnp.zeros((8, LANES), jnp.float32)
        lam_rows = lam_rows.at[0:4, 0:HEAD_DIM].set(jnp.stack([lam_q1[l], lam_k1[l], lam_q2[l], lam_k2[l]]))
        ya = _diff_attention(p3, lam_rows, diff_gain[l].reshape(1, LANES), l)
        yb = _fox_attention(p3, cum_n, cum_t)
        yc = _sb_attention(p3)
        yd = _dsa_attention(p3, small3)
        x2 = _merge(x2, mix_norm[l].reshape(1, d), w_gate[l].astype(bf), b_gate[l].reshape(1, N_BRANCH * d),
                    ya.reshape(b * s, -1), yb.reshape(b * s, -1), yc.reshape(b * s, -1), yd.reshape(b * s, -1),
                    w_br_a[l].astype(bf), w_br_b[l].astype(bf), w_br_c[l].astype(bf), w_br_d[l].astype(bf),
                    w_out[l].astype(bf))
        x2 = _ffn(x2, ffn2_norm[l].reshape(1, d), ffn2_w_gu[l][:, :D_FF].astype(bf),
                  ffn2_w_gu[l][:, D_FF:].astype(bf), ffn2_w_down[l].astype(bf), fgain,
                  final_norm=(l == depth - 1))
    return x2.reshape(b, s, d)
```

```python
import functools
import math

import numpy as np
import jax
import jax.numpy as jnp
from jax import lax
from jax.experimental import pallas as pl
from jax.experimental.pallas import tpu as pltpu

D_MODEL = 1024
HEAD_DIM = 64
N_IDX_HEADS = 8
TOPK_MAX = 256
ROPE_THETA = 500000.0
ROPE_DIM = HEAD_DIM // 4
D_FF = 2816
N_BRANCH = 4
NORM_EPS = 1e-6
SUBLN_EPS = 1e-5
SCALE = HEAD_DIM ** -0.5
LOG2E = math.log2(math.e)

LANES = 128
VMEM_LIMIT = 56 * 1024 * 1024

BLK_AQ, BLK_AK, BLK_AV = 0, 4, 8
BLK_BQ, BLK_BK, BLK_BV = 12, 14, 16
BLK_CQ, BLK_CK, BLK_CV = 18, 20, 22
BLK_DQ, BLK_DK, BLK_DV = 24, 26, 28
BLK_IK, BLK_SPARE, BLK_IQ = 30, 31, 32
N_PROJ_BLKS = 36
PROJ_W = N_PROJ_BLKS * LANES
ROPE_BLKS = frozenset(list(range(0, 8)) + [24, 25, 26, 27, 30] + list(range(32, 36)))
SM_IW, SM_F = 0, 8

NEG_BIG = -1e30
INT_MIN = -(2 ** 31)


def _cparams(sem, vmem=VMEM_LIMIT):
    return pltpu.CompilerParams(dimension_semantics=sem, vmem_limit_bytes=vmem)


def _resident(shape):
    nd = len(shape)
    return pl.BlockSpec(shape, lambda *_: (0,) * nd, pipeline_mode=pl.Buffered(1))


def _rms(x, g, eps):
    ms = jnp.mean(x * x, axis=-1, keepdims=True)
    return x * lax.rsqrt(ms + eps) * g


def _dot(a, b):
    return jnp.dot(a, b, preferred_element_type=jnp.float32)


def _dot_nt(a, b):
    return lax.dot_general(a, b, (((1,), (1,)), ((), ())), preferred_element_type=jnp.float32)


def _ffn_kernel(x_ref, g_ref, wg_ref, wu_ref, wd_ref, fg_ref, o_ref, *, tf, final_norm):
    x = x_ref[...]
    hb = _rms(x, g_ref[...], NORM_EPS).astype(jnp.bfloat16)
    acc = jnp.zeros(x.shape, jnp.float32)
    for j in range(D_FF // tf):
        sl = slice(j * tf, (j + 1) * tf)
        g = _dot(hb, wg_ref[:, sl])
        u = _dot(hb, wu_ref[:, sl])
        a = (g * jax.nn.sigmoid(g) * u).astype(jnp.bfloat16)
        acc = acc + _dot(a, wd_ref[sl, :])
    y = x + 0.5 * acc
    if final_norm:
        y = _rms(y, fg_ref[...], NORM_EPS)
    o_ref[...] = y


def _ffn(x2, gain, wg, wu, wd, fgain, *, final_norm, tm=512, tf=256):
    m, d = x2.shape
    return pl.pallas_call(
        functools.partial(_ffn_kernel, tf=tf, final_norm=final_norm),
        grid=(m // tm,),
        in_specs=[
            pl.BlockSpec((tm, d), lambda i: (i, 0)),
            _resident((1, d)),
            _resident((d, D_FF)),
            _resident((d, D_FF)),
            _resident((D_FF, d)),
            _resident((1, d)),
        ],
        out_specs=pl.BlockSpec((tm, d), lambda i: (i, 0)),
        out_shape=jax.ShapeDtypeStruct((m, d), jnp.float32),
        compiler_params=_cparams(("parallel",)),
        name="ffn",
    )(x2, gain, wg, wu, wd, fgain)


def _rope_kernel(pos_ref, freq_ref, m1_ref, m2_ref, c_ref, s1_ref, s2_ref):
    ang = pos_ref[...].astype(jnp.float32) * freq_ref[...]
    c = jnp.cos(ang)
    s = jnp.sin(ang)
    c_ref[...] = c
    s1_ref[...] = s * m1_ref[...]
    s2_ref[...] = s * m2_ref[...]


def _rope_tables(positions):
    s = positions.shape[0]
    freqs = ROPE_THETA ** (-jnp.arange(0, ROPE_DIM, 2, dtype=jnp.float32) / ROPE_DIM)
    half = ROPE_DIM // 2
    d = np.arange(LANES) % HEAD_DIM
    freq_row = jnp.where(d < ROPE_DIM, freqs[d % half], 0.0).reshape(1, LANES)
    m1 = jnp.asarray(np.where(d < half, -1.0, 0.0).astype(np.float32)).reshape(1, LANES)
    m2 = jnp.asarray(np.where((d >= half) & (d < ROPE_DIM), 1.0, 0.0).astype(np.float32)).reshape(1, LANES)
    tab = jax.ShapeDtypeStruct((s, LANES), jnp.float32)
    ts = min(s, 1024)
    row = pl.BlockSpec((1, LANES), lambda i: (0, 0))
    blk = pl.BlockSpec((ts, LANES), lambda i: (i, 0))
    return pl.pallas_call(
        _rope_kernel,
        grid=(s // ts,),
        in_specs=[pl.BlockSpec((ts, 1), lambda i: (i, 0)), row, row, row],
        out_specs=[blk, blk, blk],
        out_shape=[tab, tab, tab],
        compiler_params=_cparams(("parallel",)),
        name="rope_tables",
    )(positions.reshape(s, 1), freq_row, m1, m2)


def _proj_kernel(x_ref, g_ref, w_ref, wsm_ref, bsm_ref, c_ref, s1_ref, s2_ref, p_ref, sm_ref):
    hb = _rms(x_ref[...], g_ref[...], NORM_EPS).astype(jnp.bfloat16)
    c, s1, s2 = c_ref[...], s1_ref[...], s2_ref[...]
    chunk = 4
    for cb in range(N_PROJ_BLKS // chunk):
        y = _dot(hb, w_ref[:, cb * chunk * LANES:(cb + 1) * chunk * LANES])
        for sub in range(chunk):
            blk = cb * chunk + sub
            ys = y[:, sub * LANES:(sub + 1) * LANES]
            if blk in ROPE_BLKS:
                ys = (ys * c + pltpu.roll(ys, LANES - ROPE_DIM // 2, 1) * s1
                      + pltpu.roll(ys, ROPE_DIM // 2, 1) * s2)
            p_ref[:, blk * LANES:(blk + 1) * LANES] = ys.astype(jnp.bfloat16)
    sm_ref[...] = _dot(hb, wsm_ref[...]) + bsm_ref[...]


def _proj(x2, gain, w, wsm, bsm, c, s1, s2, seq, *, tm=512):
    m, d = x2.shape
    tm = min(tm, seq)
    nrep = seq // tm
    tab = pl.BlockSpec((tm, LANES), lambda i: (i % nrep, 0))
    return pl.pallas_call(
        _proj_kernel,
        grid=(m // tm,),
        in_specs=[
            pl.BlockSpec((tm, d), lambda i: (i, 0)),
            _resident((1, d)),
            _resident((d, PROJ_W)),
            _resident((d, LANES)),
            _resident((1, LANES)),
            tab, tab, tab,
        ],
        out_specs=[pl.BlockSpec((tm, PROJ_W), lambda i: (i, 0)),
                   pl.BlockSpec((tm, LANES), lambda i: (i, 0))],
        out_shape=[jax.ShapeDtypeStruct((m, PROJ_W), jnp.bfloat16),
                   jax.ShapeDtypeStruct((m, LANES), jnp.float32)],
        compiler_params=_cparams(("parallel",)),
        name="proj",
    )(x2, gain, w, wsm, bsm, c, s1, s2)


def _cumsum_kernel(sm_ref, cn_ref, ct_ref):
    t = sm_ref[0].T
    n = t.shape[1]
    ls = jnp.minimum(t, 0.0) - jnp.log(1.0 + jnp.exp(-jnp.abs(t)))
    lane = lax.broadcasted_iota(jnp.int32, ls.shape, 1)
    sh = 1
    while sh < n:
        ls = ls + jnp.where(lane >= sh, pltpu.roll(ls, sh, 1), 0.0)
        sh *= 2
    ls = ls * LOG2E
    ct_ref[0] = ls[SM_F:SM_F + 8, :]
    for h in range(4):
        cn_ref[0, h] = jnp.broadcast_to(ls[SM_F + h:SM_F + h + 1, :], ls.shape).T


def _cumsum(small3):
    b, s, _ = small3.shape
    return pl.pallas_call(
        _cumsum_kernel,
        grid=(b,),
        in_specs=[pl.BlockSpec((1, s, LANES), lambda i: (i, 0, 0))],
        out_specs=[pl.BlockSpec((1, 4, s, LANES), lambda i: (i, 0, 0, 0)),
                   pl.BlockSpec((1, 8, s), lambda i: (i, 0, 0))],
        out_shape=[jax.ShapeDtypeStruct((b, 4, s, LANES), jnp.float32),
                   jax.ShapeDtypeStruct((b, 8, s), jnp.float32)],
        compiler_params=_cparams(("parallel",)),
        name="forget_cumsum",
    )(small3)


def _half_masks(dtype):
    lane = lax.broadcasted_iota(jnp.int32, (1, LANES), 1)
    lo = (lane < HEAD_DIM).astype(dtype)
    return lo, (1 - lo).astype(dtype)


def _pipeline_pair(n_steps, logits, weights, pv):
    assert n_steps % 2 == 1 and n_steps >= 3
    logits(0, 0)
    logits(1, 1)
    weights(0, 0, True)
    logits(2, 0)
    weights(1, 1, True)
    pv(0, 0)

    def body(i, _):
        s = 2 * i
        logits(s + 1, 1)
        weights(s, 0, False)
        pv(s - 1, 1)
        logits(s + 2, 0)
        weights(s + 1, 1, False)
        pv(s, 0)
        return 0

    lax.fori_loop(1, n_steps // 2, body, 0)
    pv(n_steps - 2, 1)
    weights(n_steps - 1, 0, False)
    pv(n_steps - 1, 0)


def _pair_steps(qi, nq, reverse=False):
    def where(step):
        u = step - 2
        in_b = jnp.where(step < 2, step == 1, u >= qi)
        v = jnp.where(in_b, u - qi, u)
        n_other = jnp.where(in_b, nq - 1 - qi, qi)
        blk = (n_other - 1 - v) if reverse else v
        return in_b.astype(jnp.int32), jnp.where(step < 2, n_other, blk)

    return where


def _build_vt(v_ref, vt_ref, seq, chunk=512):
    chunk = min(chunk, seq)
    for c in range(seq // chunk):
        blk = v_ref[0, c * chunk:(c + 1) * chunk, :].astype(jnp.float32)
        vt_ref[:, c * chunk:(c + 1) * chunk] = blk.T.astype(vt_ref.dtype)


def _softmax_stage(s, m_ref, l_ref, p_buf, a_buf):
    m = m_ref[...]
    m_new = jnp.maximum(m, jnp.max(s, axis=0, keepdims=True))
    alpha = jnp.exp2(m - m_new)
    p = jnp.exp2(s - m_new)
    l_ref[...] = alpha * l_ref[...] + jnp.sum(p, axis=0, keepdims=True)
    m_ref[...] = m_new
    a_buf[...] = alpha
    p_buf[...] = p.astype(jnp.bfloat16)


def _init_pair_state(acc_ref, m_ref, l_ref):
    acc_ref[...] = jnp.zeros_like(acc_ref)
    m_ref[...] = jnp.full(m_ref.shape, NEG_BIG, jnp.float32)
    l_ref[...] = jnp.zeros_like(l_ref)


def _diff_kernel(qa_ref, qb_ref, k_ref, v_ref, lam_ref, gain_ref, oa_ref, ob_ref,
                 vt_ref, qs_ref, acc_ref, m_ref, l_ref,
                 s0_ref, s1_ref, p0_ref, p1_ref, a0_ref, a1_ref, *, t, lam_init, seq):
    qi = pl.program_id(2)
    nq = seq // t
    s_bufs, p_bufs, a_bufs = (s0_ref, s1_ref), (p0_ref, p1_ref), (a0_ref, a1_ref)

    @pl.when(qi == 0)
    def _():
        _build_vt(v_ref, vt_ref, seq)

    qs_ref[0] = _stack_heads(qa_ref[0], SCALE * LOG2E)
    qs_ref[1] = _stack_heads(qb_ref[0], SCALE * LOG2E)
    _init_pair_state(acc_ref, m_ref, l_ref)
    where = _pair_steps(qi, nq)

    def logits(step, slot):
        tile, blk = where(step)
        s_bufs[slot][...] = _dot_nt(k_ref[0, pl.ds(pl.multiple_of(blk * t, t), t), :], qs_ref[tile])

    def softmax(step, slot, masked):
        tile, _ = where(step)
        s = s_bufs[slot][...]
        if masked:
            s = jnp.where(_stacked_causal(t, False), s, -jnp.inf)
        _softmax_stage(s, m_ref.at[tile], l_ref.at[tile], p_bufs[slot], a_bufs[slot])

    def pv(step, slot):
        tile, blk = where(step)
        vt = vt_ref[:, pl.ds(pl.multiple_of(blk * t, t), t)]
        acc_ref[tile] = a_bufs[slot][...] * acc_ref[tile] + _dot(vt, p_bufs[slot][...])

    _pipeline_pair(nq + 1, logits, softmax, pv)

    lp = lam_ref[...]
    lam = (jnp.exp(jnp.sum(lp[0:1] * lp[1:2], axis=-1, keepdims=True))
           - jnp.exp(jnp.sum(lp[2:3] * lp[3:4], axis=-1, keepdims=True)) + lam_init)
    for tile, o_ref in enumerate((oa_ref, ob_ref)):
        yt = acc_ref[tile] / l_ref[tile]
        y = (yt[:, :t] - lam * yt[:, t:]).T
        y = _rms(y, gain_ref[...], SUBLN_EPS) * (1.0 - lam_init)
        o_ref[0] = y.astype(o_ref.dtype)


def _pair_out(b, s, width):
    return [jax.ShapeDtypeStruct((b, s // 2, width), jnp.bfloat16)] * 2


def _join_halves(lo, hi):
    return jnp.concatenate([lo, hi], axis=1)


def _diff_attention(p3, lam_rows, gain, layer_idx, *, t=256):
    b, s, _ = p3.shape
    t = min(t, s // 2)
    nq = s // t
    lam_init = 0.8 - 0.6 * math.exp(-0.3 * layer_idx)
    nh = 4
    return _join_halves(*pl.pallas_call(
        functools.partial(_diff_kernel, t=t, lam_init=lam_init, seq=s),
        grid=(b, nh, nq // 2),
        in_specs=[
            pl.BlockSpec((1, t, LANES), lambda bi, h, qi: (bi, qi, BLK_AQ + h)),
            pl.BlockSpec((1, t, LANES), lambda bi, h, qi: (bi, nq - 1 - qi, BLK_AQ + h)),
            pl.BlockSpec((1, s, LANES), lambda bi, h, qi: (bi, 0, BLK_AK + h)),
            pl.BlockSpec((1, s, LANES), lambda bi, h, qi: (bi, 0, BLK_AV + h)),
            pl.BlockSpec((8, LANES), lambda bi, h, qi: (0, 0)),
            pl.BlockSpec((1, LANES), lambda bi, h, qi: (0, 0)),
        ],
        out_specs=[pl.BlockSpec((1, t, LANES), lambda bi, h, qi: (bi, qi, h)),
                   pl.BlockSpec((1, t, LANES), lambda bi, h, qi: (bi, nq // 2 - 1 - qi, h))],
        out_shape=_pair_out(b, s, nh * LANES),
        scratch_shapes=_flash_t_scratch(s, t),
        compiler_params=_cparams(("arbitrary", "arbitrary", "arbitrary")),
        name="diff_attn",
    )(p3, p3, p3, p3, lam_rows, gain))


def _stack_heads(q, scale):
    lo, hi = _half_masks(jnp.float32)
    q = q.astype(jnp.float32) * scale
    return jnp.concatenate([(q * lo).astype(jnp.bfloat16), (q * hi).astype(jnp.bfloat16)], axis=0)


def _stacked_causal(t, strict):
    kr = lax.broadcasted_iota(jnp.int32, (t, 2 * t), 0)
    qc = lax.broadcasted_iota(jnp.int32, (t, 2 * t), 1)
    qc = jnp.where(qc >= t, qc - t, qc)
    return (kr < qc) if strict else (kr <= qc)


def _unstack_heads_t(acc, t):
    return jnp.concatenate([acc[:HEAD_DIM, :t], acc[HEAD_DIM:, t:]], axis=0).T


def _fox_kernel(qa_ref, qb_ref, k_ref, v_ref, cr_ref, ct_ref, oa_ref, ob_ref,
                vt_ref, qs_ref, acc_ref, m_ref, l_ref,
                s0_ref, s1_ref, p0_ref, p1_ref, a0_ref, a1_ref, cq_ref, *, t, seq):
    pair = pl.program_id(1)
    qi = pl.program_id(2)
    nq = seq // t
    s_bufs, p_bufs, a_bufs = (s0_ref, s1_ref), (p0_ref, p1_ref), (a0_ref, a1_ref)

    @pl.when(qi == 0)
    def _():
        _build_vt(v_ref, vt_ref, seq)

    for tile, (q_ref, row) in enumerate(((qa_ref, qi), (qb_ref, nq - 1 - qi))):
        qs_ref[tile] = _stack_heads(q_ref[0], SCALE * LOG2E)
        qoff = pl.multiple_of(row * t, t)
        cq_ref[tile] = jnp.concatenate(
            [ct_ref[0, pl.ds(2 * pair + i, 1), pl.ds(qoff, t)] for i in range(2)], axis=1)
    _init_pair_state(acc_ref, m_ref, l_ref)
    where = _pair_steps(qi, nq)

    def logits(step, slot):
        tile, blk = where(step)
        s_bufs[slot][...] = _dot_nt(k_ref[0, pl.ds(pl.multiple_of(blk * t, t), t), :], qs_ref[tile])

    def softmax(step, slot, masked):
        tile, blk = where(step)
        off = pl.multiple_of(blk * t, t)
        ck = jnp.concatenate([jnp.tile(cr_ref[0, i, pl.ds(off, t), :], (1, t // LANES)) for i in range(2)],
                             axis=1)
        s = (s_bufs[slot][...] + cq_ref[tile]) - ck
        if masked:
            s = jnp.where(_stacked_causal(t, False), s, -jnp.inf)
        _softmax_stage(s, m_ref.at[tile], l_ref.at[tile], p_bufs[slot], a_bufs[slot])

    def pv(step, slot):
        tile, blk = where(step)
        vt = vt_ref[:, pl.ds(pl.multiple_of(blk * t, t), t)]
        acc_ref[tile] = a_bufs[slot][...] * acc_ref[tile] + _dot(vt, p_bufs[slot][...])

    _pipeline_pair(nq + 1, logits, softmax, pv)

    for tile, o_ref in enumerate((oa_ref, ob_ref)):
        o_ref[0] = _unstack_heads_t(acc_ref[tile] / l_ref[tile], t).astype(o_ref.dtype)


def _flash_t_scratch(s, t):
    row = pltpu.VMEM((1, 2 * t), jnp.float32)
    rows = pltpu.VMEM((2, 1, 2 * t), jnp.float32)
    return [pltpu.VMEM((LANES, s), jnp.bfloat16),
            pltpu.VMEM((2, 2 * t, LANES), jnp.bfloat16),
            pltpu.VMEM((2, LANES, 2 * t), jnp.float32),
            rows, rows,
            pltpu.VMEM((t, 2 * t), jnp.float32), pltpu.VMEM((t, 2 * t), jnp.float32),
            pltpu.VMEM((t, 2 * t), jnp.bfloat16), pltpu.VMEM((t, 2 * t), jnp.bfloat16),
            row, row]


def _fox_attention(p3, cum_rep, cum_t, *, t=256):
    b, s, _ = p3.shape
    t = min(t, s // 2)
    nq = s // t
    return _join_halves(*pl.pallas_call(
        functools.partial(_fox_kernel, t=t, seq=s),
        grid=(b, 2, nq // 2),
        in_specs=[
            pl.BlockSpec((1, t, LANES), lambda bi, p, qi: (bi, qi, BLK_BQ + p)),
            pl.BlockSpec((1, t, LANES), lambda bi, p, qi: (bi, nq - 1 - qi, BLK_BQ + p)),
            pl.BlockSpec((1, s, LANES), lambda bi, p, qi: (bi, 0, BLK_BK + p)),
            pl.BlockSpec((1, s, LANES), lambda bi, p, qi: (bi, 0, BLK_BV + p)),
            pl.BlockSpec((1, 2, s, LANES), lambda bi, p, qi: (bi, p, 0, 0)),
            pl.BlockSpec((1, 8, s), lambda bi, p, qi: (bi, 0, 0)),
        ],
        out_specs=[pl.BlockSpec((1, t, LANES), lambda bi, p, qi: (bi, qi, p)),
                   pl.BlockSpec((1, t, LANES), lambda bi, p, qi: (bi, nq // 2 - 1 - qi, p))],
        out_shape=_pair_out(b, s, 2 * LANES),
        scratch_shapes=_flash_t_scratch(s, t) + [pltpu.VMEM((2, 1, 2 * t), jnp.float32)],
        compiler_params=_cparams(("arbitrary", "arbitrary", "arbitrary")),
        name="fox_attn",
    )(p3, p3, p3, p3, cum_rep, cum_t))


def _sb_kernel(qa_ref, qb_ref, k_ref, v_ref, oa_ref, ob_ref,
               vt_ref, qs_ref, acc_ref, run_ref,
               s0_ref, s1_ref, p0_ref, p1_ref, *, t, seq):
    qi = pl.program_id(2)
    nq = seq // t
    s_bufs, p_bufs = (s0_ref, s1_ref), (p0_ref, p1_ref)

    @pl.when(qi == 0)
    def _():
        _build_vt(v_ref, vt_ref, seq)

    qs_ref[0] = _stack_heads(qa_ref[0], SCALE * LOG2E)
    qs_ref[1] = _stack_heads(qb_ref[0], SCALE * LOG2E)
    r = lax.broadcasted_iota(jnp.int32, (t, 2 * t), 0)
    c = lax.broadcasted_iota(jnp.int32, (t, 2 * t), 1)
    later2 = (jnp.where(c >= t, c - t, c) > r).astype(jnp.bfloat16)
    acc_ref[...] = jnp.zeros_like(acc_ref)
    run_ref[...] = jnp.zeros_like(run_ref)
    where = _pair_steps(qi, nq, reverse=True)

    def logits(step, slot):
        tile, blk = where(step)
        s_bufs[slot][...] = _dot_nt(k_ref[0, pl.ds(pl.multiple_of(blk * t, t), t), :], qs_ref[tile])

    def weights(step, slot, masked):
        tile, _ = where(step)
        z = s_bufs[slot][...]
        log_beta = jnp.minimum(z, 0.0) - jnp.log2(1.0 + jnp.exp2(-jnp.abs(z)))
        l1m = log_beta - z
        if masked:
            cm = _stacked_causal(t, True)
            l1m = jnp.where(cm, l1m, 0.0)
        l_hi = l1m.astype(jnp.bfloat16)
        l_lo = (l1m - l_hi.astype(jnp.float32)).astype(jnp.bfloat16)
        after = _dot(later2, jnp.concatenate([l_hi, l_lo], axis=0)) + run_ref[tile]
        a = jnp.exp2(log_beta + after)
        if masked:
            a = jnp.where(cm, a, 0.0)
        p_bufs[slot][...] = a.astype(jnp.bfloat16)
        run_ref[tile] += jnp.sum(l1m, axis=0, keepdims=True)

    def pv(step, slot):
        tile, blk = where(step)
        acc_ref[tile] += _dot(vt_ref[:, pl.ds(pl.multiple_of(blk * t, t), t)], p_bufs[slot][...])

    _pipeline_pair(nq + 1, logits, weights, pv)

    for tile, o_ref in enumerate((oa_ref, ob_ref)):
        o_ref[0] = _unstack_heads_t(acc_ref[tile], t).astype(o_ref.dtype)


def _sb_attention(p3, *, t=256):
    b, s, _ = p3.shape
    t = min(t, s // 2)
    nq = s // t
    return _join_halves(*pl.pallas_call(
        functools.partial(_sb_kernel, t=t, seq=s),
        grid=(b, 2, nq // 2),
        in_specs=[
            pl.BlockSpec((1, t, LANES), lambda bi, p, qi: (bi, qi, BLK_CQ + p)),
            pl.BlockSpec((1, t, LANES), lambda bi, p, qi: (bi, nq - 1 - qi, BLK_CQ + p)),
            pl.BlockSpec((1, s, LANES), lambda bi, p, qi: (bi, 0, BLK_CK + p)),
            pl.BlockSpec((1, s, LANES), lambda bi, p, qi: (bi, 0, BLK_CV + p)),
        ],
        out_specs=[pl.BlockSpec((1, t, LANES), lambda bi, p, qi: (bi, qi, p)),
                   pl.BlockSpec((1, t, LANES), lambda bi, p, qi: (bi, nq // 2 - 1 - qi, p))],
        out_shape=_pair_out(b, s, 2 * LANES),
        scratch_shapes=[pltpu.VMEM((LANES, s), jnp.bfloat16),
                        pltpu.VMEM((2, 2 * t, LANES), jnp.bfloat16),
                        pltpu.VMEM((2, LANES, 2 * t), jnp.float32),
                        pltpu.VMEM((2, 1, 2 * t), jnp.float32),
                        pltpu.VMEM((t, 2 * t), jnp.float32), pltpu.VMEM((t, 2 * t), jnp.float32),
                        pltpu.VMEM((t, 2 * t), jnp.bfloat16), pltpu.VMEM((t, 2 * t), jnp.bfloat16)],
        compiler_params=_cparams(("arbitrary", "arbitrary", "arbitrary")),
        name="sb_attn",
    )(p3, p3, p3, p3))


def _dsa_kernel(iq_ref, q_ref, sm_ref, ik_ref, k_ref, v_ref, o_ref,
                keys_ref, bias_ref, vt_ref, sig_ref, acc_ref, m_ref, l_ref,
                s0_ref, s1_ref, *, tq, ck, topk, seq):
    qi = pl.program_id(1)
    nblk = (qi * tq + tq + ck - 1) // ck
    lo, hi = _half_masks(jnp.float32)
    halves = (lo, hi)

    @pl.when(qi == 0)
    def _():
        _build_vt(v_ref, vt_ref, seq)

    iw_t = sm_ref[0].T[SM_IW:SM_IW + N_IDX_HEADS, :]
    iq = iq_ref[0].astype(jnp.float32)
    iq_stack = jnp.concatenate(
        [(iq[:, (h // 2) * LANES:(h // 2 + 1) * LANES] * halves[h % 2]).astype(jnp.bfloat16)
         for h in range(N_IDX_HEADS)], axis=0)
    qds = [_stack_heads(q_ref[0][:, pr * LANES:(pr + 1) * LANES], SCALE * LOG2E) for pr in range(2)]
    t_idx = qi * tq + lax.broadcasted_iota(jnp.int32, (1, tq), 1)
    row = lax.broadcasted_iota(jnp.int32, (ck, tq), 0)

    def score_chunk(c, _):
        off = pl.multiple_of(c * ck, ck)
        ikc = ik_ref[0, pl.ds(off, ck), :]
        rel = _dot_nt(ikc, iq_stack)
        acc = jnp.zeros((ck, tq), jnp.float32)
        for h in range(N_IDX_HEADS):
            acc = acc + iw_t[h:h + 1, :] * jnp.maximum(rel[:, h * tq:(h + 1) * tq], 0.0)
        bits = pltpu.bitcast(acc, jnp.int32)
        key = bits ^ ((bits >> 31) & 0x7FFFFFFF)
        keys_ref[pl.ds(off, ck), :] = jnp.where(row + off <= t_idx, key, INT_MIN)
        return 0

    lax.fori_loop(0, nblk, score_chunk, 0)

    def count(pred_fn):
        def body(c, cnt):
            off = pl.multiple_of(c * ck, ck)
            hit = pred_fn(keys_ref[pl.ds(off, ck), :], off).astype(jnp.int32)
            lanes = 4
            accs = [hit[g * 8:(g + 1) * 8] for g in range(lanes)]
            for g in range(lanes, ck // 8):
                accs[g % lanes] = accs[g % lanes] + hit[g * 8:(g + 1) * 8]
            return cnt + ((accs[0] + accs[1]) + (accs[2] + accs[3]))
        cnt8 = lax.fori_loop(0, nblk, body, jnp.zeros((8, tq), jnp.int32))
        return jnp.sum(cnt8, axis=0, keepdims=True)

    c0 = count(lambda kk, off: kk >= 0)
    tau0 = jnp.where(c0 >= topk, 0, INT_MIN).astype(jnp.int32)
    cnt0 = jnp.where(c0 >= topk, c0, nblk * ck).astype(jnp.int32)

    def bit_step(i, carry):
        tau, cnt_ge = carry
        cand = tau + jnp.left_shift(jnp.int32(1), 30 - i)
        cnt = count(lambda kk, off: kk >= cand)
        ok = cnt >= topk
        return jnp.where(ok, cand, tau), jnp.where(ok, cnt, cnt_ge)

    tau, cnt_ge = lax.fori_loop(0, 31, bit_step, (tau0, cnt0))
    real = tau != INT_MIN
    sig_ref[...] = jnp.where(real, seq, -1).astype(jnp.int32)
    over = jnp.max(jnp.where(real & (cnt_ge > topk), 1, 0))

    @pl.when(over > 0)
    def _():
        cnt_gt = count(lambda kk, off: kk > tau)
        need = topk - cnt_gt

        def idx_step(i, x):
            cand = x + jnp.left_shift(jnp.int32(1), (seq.bit_length() - 2) - i)
            f = count(lambda kk, off: (kk == tau) & (row + off < cand))
            return jnp.where(f < need, cand, x)

        x = lax.fori_loop(0, seq.bit_length() - 1, idx_step, jnp.zeros((1, tq), jnp.int32))
        sig_ref[...] = jnp.where(real, x, -1)

    sigma = sig_ref[...]

    def bias_chunk(c, _):
        off = pl.multiple_of(c * ck, ck)
        kk = keys_ref[pl.ds(off, ck), :]
        sel = (kk > tau) | ((kk == tau) & (row + off <= sigma))
        bias_ref[pl.ds(off, ck), :] = jnp.where(sel, 0.0, -jnp.inf)
        return 0

    lax.fori_loop(0, nblk, bias_chunk, 0)

    s_bufs = (s0_ref, s1_ref)
    _init_pair_state(acc_ref, m_ref, l_ref)

    def logits(c, slot):
        off = pl.multiple_of(c * ck, ck)
        for pr in range(2):
            s_bufs[slot][:, 2 * pr * tq:2 * (pr + 1) * tq] = _dot_nt(
                k_ref[0, pl.ds(off, ck), pr * LANES:(pr + 1) * LANES], qds[pr])

    def consume(c, slot):
        off = pl.multiple_of(c * ck, ck)
        s = s_bufs[slot][...] + jnp.tile(bias_ref[pl.ds(off, ck), :], (1, 4))
        m = m_ref[...]
        m_new = jnp.maximum(m, jnp.max(s, axis=0, keepdims=True))
        alpha = jnp.exp2(m - m_new)
        p = jnp.exp2(s - m_new)
        l_ref[...] = alpha * l_ref[...] + jnp.sum(p, axis=0, keepdims=True)
        m_ref[...] = m_new
        p = p.astype(jnp.bfloat16)
        upd = jnp.concatenate(
            [_dot(vt_ref[h * HEAD_DIM:(h + 1) * HEAD_DIM, pl.ds(off, ck)], p[:, h * tq:(h + 1) * tq])
             for h in range(4)], axis=1)
        acc_ref[...] = alpha * acc_ref[...] + upd

    def pair_step(i, _):
        c = 2 * i
        logits(c + 1, 1)
        consume(c, 0)
        logits(jnp.minimum(c + 2, nblk - 1), 0)
        consume(c + 1, 1)
        return 0

    logits(0, 0)
    lax.fori_loop(0, nblk // 2, pair_step, 0)

    @pl.when(nblk % 2 == 1)
    def _():
        consume(nblk - 1, 0)

    out = acc_ref[...] / l_ref[...]
    out_t = jnp.concatenate([out[:, h * tq:(h + 1) * tq] for h in range(4)], axis=0)
    o_ref[0] = out_t.T.astype(o_ref.dtype)


def _dsa_attention(p3, small3, *, tq=256, ck=256):
    b, s, _ = p3.shape
    ck = min(ck, s)
    topk = min(TOPK_MAX, s // 4)
    return pl.pallas_call(
        functools.partial(_dsa_kernel, tq=tq, ck=ck, topk=topk, seq=s),
        grid=(b, s // tq),
        in_specs=[
            pl.BlockSpec((1, tq, 4 * LANES), lambda bi, qi: (bi, qi, BLK_IQ // 4)),
            pl.BlockSpec((1, tq, 2 * LANES), lambda bi, qi: (bi, qi, BLK_DQ // 2)),
            pl.BlockSpec((1, tq, LANES), lambda bi, qi: (bi, qi, 0)),
            pl.BlockSpec((1, s, LANES), lambda bi, qi: (bi, 0, BLK_IK)),
            pl.BlockSpec((1, s, 2 * LANES), lambda bi, qi: (bi, 0, BLK_DK // 2)),
            pl.BlockSpec((1, s, 2 * LANES), lambda bi, qi: (bi, 0, BLK_DV // 2)),
        ],
        out_specs=pl.BlockSpec((1, tq, 2 * LANES), lambda bi, qi: (bi, qi, 0)),
        out_shape=jax.ShapeDtypeStruct((b, s, 2 * LANES), jnp.bfloat16),
        scratch_shapes=[
            pltpu.VMEM((s, tq), jnp.int32),
            pltpu.VMEM((s, tq), jnp.float32),
            pltpu.VMEM((4 * HEAD_DIM, s), jnp.bfloat16),
            pltpu.VMEM((1, tq), jnp.int32),
            pltpu.VMEM((HEAD_DIM, 4 * tq), jnp.float32),
            pltpu.VMEM((1, 4 * tq), jnp.float32), pltpu.VMEM((1, 4 * tq), jnp.float32),
            pltpu.VMEM((ck, 4 * tq), jnp.float32), pltpu.VMEM((ck, 4 * tq), jnp.float32),
        ],
        compiler_params=_cparams(("arbitrary", "arbitrary")),
        name="dsa_attn",
    )(p3, p3, small3, p3, p3, p3)


def _merge_kernel(x_ref, g_ref, wgate_ref, bgate_ref, ya_ref, yb_ref, yc_ref, yd_ref,
                  wa_ref, wb_ref, wc_ref, wd_ref, wo_ref, o_ref):
    x = x_ref[...]
    d = x.shape[1]
    hb = _rms(x, g_ref[...], NORM_EPS).astype(jnp.bfloat16)
    merged = jnp.zeros(x.shape, jnp.float32)
    for i, (y_ref, w_ref) in enumerate(((ya_ref, wa_ref), (yb_ref, wb_ref), (yc_ref, wc_ref), (yd_ref, wd_ref))):
        gate = jax.nn.sigmoid(_dot(hb, wgate_ref[:, i * d:(i + 1) * d]) + bgate_ref[:, i * d:(i + 1) * d])
        merged = merged + gate * _dot(y_ref[...], w_ref[...])
    o_ref[...] = x + _dot(merged.astype(jnp.bfloat16), wo_ref[...])


def _merge(x2, gain, wgate, bgate, ya, yb, yc, yd, wa, wb, wc, wd, wo, *, tm=512):
    m, d = x2.shape
    rowblk = lambda w: pl.BlockSpec((tm, w), lambda i: (i, 0))
    return pl.pallas_call(
        _merge_kernel,
        grid=(m // tm,),
        in_specs=[
            rowblk(d), _resident((1, d)), _resident((d, N_BRANCH * d)), _resident((1, N_BRANCH * d)),
            rowblk(ya.shape[1]), rowblk(yb.shape[1]), rowblk(yc.shape[1]), rowblk(yd.shape[1]),
            _resident(wa.shape), _resident(wb.shape), _resident(wc.shape), _resident(wd.shape),
            _resident(wo.shape),
        ],
        out_specs=rowblk(d),
        out_shape=jax.ShapeDtypeStruct((m, d), jnp.float32),
        compiler_params=_cparams(("parallel",)),
        name="merge",
    )(x2, gain, wgate, bgate, ya, yb, yc, yd, wa, wb, wc, wd, wo)


def _proj_weights(w_in, b_fgt):
    d = w_in.shape[0]
    o = 0

    def take(width):
        nonlocal o
        seg = w_in[:, o:o + width]
        o += width
        return seg

    aq, ak, av = take(512), take(512), take(512)
    bq, bk, bv, bf = take(256), take(256), take(256), take(4)
    cq, ck, cv = take(256), take(256), take(256)
    dq, dk, dv = take(256), take(256), take(256)
    diq, dik, diw = take(512), take(64), take(8)
    zeros = jnp.zeros((d, LANES), w_in.dtype)
    big = jnp.concatenate([aq, ak, av, bq, bk, bv, cq, ck, cv, dq, dk, dv, dik, dik, zeros, diq], axis=1)
    small = jnp.concatenate([diw, bf, jnp.zeros((d, LANES - 12), w_in.dtype)], axis=1)
    bias = jnp.concatenate([jnp.zeros((8,), jnp.float32), b_fgt.astype(jnp.float32),
                            jnp.zeros((LANES - 12,), jnp.float32)]).reshape(1, LANES)
    return big.astype(jnp.bfloat16), small.astype(jnp.bfloat16), bias


def kernel(x, positions, ffn1_norm, ffn1_w_gu, ffn1_w_down, mix_norm, w_in, b_fgt, lam_q1, lam_k1, lam_q2, lam_k2, diff_gain, w_gate, b_gate, w_br_a, w_br_b, w_br_c, w_br_d, w_out, ffn2_norm, ffn2_w_gu, ffn2_w_down, final_norm):
    b, s, d = x.shape
    depth = w_in.shape[0]
    bf = jnp.bfloat16
    cos_t, sin1_t, sin2_t = _rope_tables(positions)
    fgain = final_norm.reshape(1, d)
    x2 = x.reshape(b * s, d)
    for l in range(depth):
        x2 = _ffn(x2, ffn1_norm[l].reshape(1, d), ffn1_w_gu[l][:, :D_FF].astype(bf),
                  ffn1_w_gu[l][:, D_FF:].astype(bf), ffn1_w_down[l].astype(bf), fgain, final_norm=False)
        wbig, wsmall, bsmall = _proj_weights(w_in[l], b_fgt[l])
        p2, small2 = _proj(x2, mix_norm[l].reshape(1, d), wbig, wsmall, bsmall, cos_t, sin1_t, sin2_t, s)
        p3 = p2.reshape(b, s, PROJ_W)
        small3 = small2.reshape(b, s, LANES)
        cum_n, cum_t = _cumsum(small3)
        lam_rows = jnp.zeros((8, LANES), jnp.float32)
        lam_rows = lam_rows.at[0:4, 0:HEAD_DIM].set(jnp.stack([lam_q1[l], lam_k1[l], lam_q2[l], lam_k2[l]]))
        ya = _diff_attention(p3, lam_rows, diff_gain[l].reshape(1, LANES), l)
        yb = _fox_attention(p3, cum_n, cum_t)
        yc = _sb_attention(p3)
        yd = _dsa_attention(p3, small3)
        x2 = _merge(x2, mix_norm[l].reshape(1, d), w_gate[l].astype(bf), b_gate[l].reshape(1, N_BRANCH * d),
                    ya.reshape(b * s, -1), yb.reshape(b * s, -1), yc.reshape(b * s, -1), yd.reshape(b * s, -1),
                    w_br_a[l].astype(bf), w_br_b[l].astype(bf), w_br_c[l].astype(bf), w_br_d[l].astype(bf),
                    w_out[l].astype(bf))
        x2 = _ffn(x2, ffn2_norm[l].reshape(1, d), ffn2_w_gu[l][:, :D_FF].astype(bf),
                  ffn2_w_gu[l][:, D_FF:].astype(bf), ffn2_w_down[l].astype(bf), fgain,
                  final_norm=(l == depth - 1))
    return x2.reshape(b, s, d)
```

```python
import functools
import math

import numpy as np
import jax
import jax.numpy as jnp
from jax import lax
from jax.experimental import pallas as pl
from jax.experimental.pallas import tpu as pltpu

D_MODEL = 1024
HEAD_DIM = 64
N_IDX_HEADS = 8
TOPK_MAX = 256
ROPE_THETA = 500000.0
ROPE_DIM = HEAD_DIM // 4
D_FF = 2816
N_BRANCH = 4
NORM_EPS = 1e-6
SUBLN_EPS = 1e-5
SCALE = HEAD_DIM ** -0.5
LOG2E = math.log2(math.e)

LANES = 128
VMEM_LIMIT = 56 * 1024 * 1024

BLK_AQ, BLK_AK, BLK_AV = 0, 4, 8
BLK_BQ, BLK_BK, BLK_BV = 12, 14, 16
BLK_CQ, BLK_CK, BLK_CV = 18, 20, 22
BLK_DQ, BLK_DK, BLK_DV = 24, 26, 28
BLK_IK, BLK_SPARE, BLK_IQ = 30, 31, 32
N_PROJ_BLKS = 36
PROJ_W = N_PROJ_BLKS * LANES
ROPE_BLKS = frozenset(list(range(0, 8)) + [24, 25, 26, 27, 30] + list(range(32, 36)))
SM_IW, SM_F = 0, 8

NEG_BIG = -1e30
INT_MIN = -(2 ** 31)


def _cparams(sem, vmem=VMEM_LIMIT):
    return pltpu.CompilerParams(dimension_semantics=sem, vmem_limit_bytes=vmem)


def _resident(shape):
    nd = len(shape)
    return pl.BlockSpec(shape, lambda *_: (0,) * nd, pipeline_mode=pl.Buffered(1))


def _rms(x, g, eps):
    ms = jnp.mean(x * x, axis=-1, keepdims=True)
    return x * lax.rsqrt(ms + eps) * g


def _dot(a, b):
    return jnp.dot(a, b, preferred_element_type=jnp.float32)


def _dot_nt(a, b):
    return lax.dot_general(a, b, (((1,), (1,)), ((), ())), preferred_element_type=jnp.float32)


def _ffn_kernel(x_ref, g_ref, wg_ref, wu_ref, wd_ref, fg_ref, o_ref, *, tf, final_norm):
    x = x_ref[...]
    hb = _rms(x, g_ref[...], NORM_EPS).astype(jnp.bfloat16)
    acc = jnp.zeros(x.shape, jnp.float32)
    for j in range(D_FF // tf):
        sl = slice(j * tf, (j + 1) * tf)
        g = _dot(hb, wg_ref[:, sl])
        u = _dot(hb, wu_ref[:, sl])
        a = (g * jax.nn.sigmoid(g) * u).astype(jnp.bfloat16)
        acc = acc + _dot(a, wd_ref[sl, :])
    y = x + 0.5 * acc
    if final_norm:
        y = _rms(y, fg_ref[...], NORM_EPS)
    o_ref[...] = y


def _ffn(x2, gain, wg, wu, wd, fgain, *, final_norm, tm=512, tf=256):
    m, d = x2.shape
    return pl.pallas_call(
        functools.partial(_ffn_kernel, tf=tf, final_norm=final_norm),
        grid=(m // tm,),
        in_specs=[
            pl.BlockSpec((tm, d), lambda i: (i, 0)),
            _resident((1, d)),
            _resident((d, D_FF)),
            _resident((d, D_FF)),
            _resident((D_FF, d)),
            _resident((1, d)),
        ],
        out_specs=pl.BlockSpec((tm, d), lambda i: (i, 0)),
        out_shape=jax.ShapeDtypeStruct((m, d), jnp.float32),
        compiler_params=_cparams(("parallel",)),
        name="ffn",
    )(x2, gain, wg, wu, wd, fgain)


def _rope_kernel(pos_ref, freq_ref, m1_ref, m2_ref, c_ref, s1_ref, s2_ref):
    ang = pos_ref[...].astype(jnp.float32) * freq_ref[...]
    c = jnp.cos(ang)
    s = jnp.sin(ang)
    c_ref[...] = c
    s1_ref[...] = s * m1_ref[...]
    s2_ref[...] = s * m2_ref[...]


def _rope_tables(positions):
    s = positions.shape[0]
    freqs = ROPE_THETA ** (-jnp.arange(0, ROPE_DIM, 2, dtype=jnp.float32) / ROPE_DIM)
    half = ROPE_DIM // 2
    d = np.arange(LANES) % HEAD_DIM
    freq_row = jnp.where(d < ROPE_DIM, freqs[d % half], 0.0).reshape(1, LANES)
    m1 = jnp.asarray(np.where(d < half, -1.0, 0.0).astype(np.float32)).reshape(1, LANES)
    m2 = jnp.asarray(np.where((d >= half) & (d < ROPE_DIM), 1.0, 0.0).astype(np.float32)).reshape(1, LANES)
    tab = jax.ShapeDtypeStruct((s, LANES), jnp.float32)
    ts = min(s, 1024)
    row = pl.BlockSpec((1, LANES), lambda i: (0, 0))
    blk = pl.BlockSpec((ts, LANES), lambda i: (i, 0))
    return pl.pallas_call(
        _rope_kernel,
        grid=(s // ts,),
        in_specs=[pl.BlockSpec((ts, 1), lambda i: (i, 0)), row, row, row],
        out_specs=[blk, blk, blk],
        out_shape=[tab, tab, tab],
        compiler_params=_cparams(("parallel",)),
        name="rope_tables",
    )(positions.reshape(s, 1), freq_row, m1, m2)


def _proj_kernel(x_ref, g_ref, w_ref, wsm_ref, bsm_ref, c_ref, s1_ref, s2_ref, p_ref, sm_ref):
    hb = _rms(x_ref[...], g_ref[...], NORM_EPS).astype(jnp.bfloat16)
    c, s1, s2 = c_ref[...], s1_ref[...], s2_ref[...]
    chunk = 4
    for cb in range(N_PROJ_BLKS // chunk):
        y = _dot(hb, w_ref[:, cb * chunk * LANES:(cb + 1) * chunk * LANES])
        for sub in range(chunk):
            blk = cb * chunk + sub
            ys = y[:, sub * LANES:(sub + 1) * LANES]
            if blk in ROPE_BLKS:
                ys = (ys * c + pltpu.roll(ys, LANES - ROPE_DIM // 2, 1) * s1
                      + pltpu.roll(ys, ROPE_DIM // 2, 1) * s2)
            p_ref[:, blk * LANES:(blk + 1) * LANES] = ys.astype(jnp.bfloat16)
    sm_ref[...] = _dot(hb, wsm_ref[...]) + bsm_ref[...]


def _proj(x2, gain, w, wsm, bsm, c, s1, s2, seq, *, tm=512):
    m, d = x2.shape
    tm = min(tm, seq)
    nrep = seq // tm
    tab = pl.BlockSpec((tm, LANES), lambda i: (i % nrep, 0))
    return pl.pallas_call(
        _proj_kernel,
        grid=(m // tm,),
        in_specs=[
            pl.BlockSpec((tm, d), lambda i: (i, 0)),
            _resident((1, d)),
            _resident((d, PROJ_W)),
            _resident((d, LANES)),
            _resident((1, LANES)),
            tab, tab, tab,
        ],
        out_specs=[pl.BlockSpec((tm, PROJ_W), lambda i: (i, 0)),
                   pl.BlockSpec((tm, LANES), lambda i: (i, 0))],
        out_shape=[jax.ShapeDtypeStruct((m, PROJ_W), jnp.bfloat16),
                   jax.ShapeDtypeStruct((m, LANES), jnp.float32)],
        compiler_params=_cparams(("parallel",)),
        name="proj",
    )(x2, gain, w, wsm, bsm, c, s1, s2)


def _cumsum_kernel(sm_ref, cn_ref, ct_ref):
    t = sm_ref[0].T
    n = t.shape[1]
    ls = jnp.minimum(t, 0.0) - jnp.log(1.0 + jnp.exp(-jnp.abs(t)))
    lane = lax.broadcasted_iota(jnp.int32, ls.shape, 1)
    sh = 1
    while sh < n:
        ls = ls + jnp.where(lane >= sh, pltpu.roll(ls, sh, 1), 0.0)
        sh *= 2
    ls = ls * LOG2E
    ct_ref[0] = ls[SM_F:SM_F + 8, :]
    for h in range(4):
        cn_ref[0, h] = jnp.broadcast_to(ls[SM_F + h:SM_F + h + 1, :], ls.shape).T


def _cumsum(small3):
    b, s, _ = small3.shape
    return pl.pallas_call(
        _cumsum_kernel,
        grid=(b,),
        in_specs=[pl.BlockSpec((1, s, LANES), lambda i: (i, 0, 0))],
        out_specs=[pl.BlockSpec((1, 4, s, LANES), lambda i: (i, 0, 0, 0)),
                   pl.BlockSpec((1, 8, s), lambda i: (i, 0, 0))],
        out_shape=[jax.ShapeDtypeStruct((b, 4, s, LANES), jnp.float32),
                   jax.ShapeDtypeStruct((b, 8, s), jnp.float32)],
        compiler_params=_cparams(("parallel",)),
        name="forget_cumsum",
    )(small3)


def _half_masks(dtype):
    lane = lax.broadcasted_iota(jnp.int32, (1, LANES), 1)
    lo = (lane < HEAD_DIM).astype(dtype)
    return lo, (1 - lo).astype(dtype)


def _pipeline_pair(n_steps, logits, weights, pv):
    assert n_steps % 2 == 1 and n_steps >= 3
    logits(0, 0)
    logits(1, 1)
    weights(0, 0, True)
    logits(2, 0)
    weights(1, 1, True)
    pv(0, 0)

    def body(i, _):
        s = 2 * i
        logits(s + 1, 1)
        weights(s, 0, False)
        pv(s - 1, 1)
        logits(s + 2, 0)
        weights(s + 1, 1, False)
        pv(s, 0)
        return 0

    lax.fori_loop(1, n_steps // 2, body, 0)
    pv(n_steps - 2, 1)
    weights(n_steps - 1, 0, False)
    pv(n_steps - 1, 0)


def _pair_steps(qi, nq, reverse=False):
    def where(step):
        u = step - 2
        in_b = jnp.where(step < 2, step == 1, u >= qi)
        v = jnp.where(in_b, u - qi, u)
        n_other = jnp.where(in_b, nq - 1 - qi, qi)
        blk = (n_other - 1 - v) if reverse else v
        return in_b.astype(jnp.int32), jnp.where(step < 2, n_other, blk)

    return where


def _build_vt(v_ref, vt_ref, seq, chunk=512):
    chunk = min(chunk, seq)
    for c in range(seq // chunk):
        blk = v_ref[0, c * chunk:(c + 1) * chunk, :].astype(jnp.float32)
        vt_ref[:, c * chunk:(c + 1) * chunk] = blk.T.astype(vt_ref.dtype)


def _softmax_stage(s, m_ref, l_ref, p_buf, a_buf):
    m = m_ref[...]
    m_new = jnp.maximum(m, jnp.max(s, axis=0, keepdims=True))
    alpha = jnp.exp2(m - m_new)
    p = jnp.exp2(s - m_new)
    l_ref[...] = alpha * l_ref[...] + jnp.sum(p, axis=0, keepdims=True)
    m_ref[...] = m_new
    a_buf[...] = alpha
    p_buf[...] = p.astype(jnp.bfloat16)


def _init_pair_state(acc_ref, m_ref, l_ref):
    acc_ref[...] = jnp.zeros_like(acc_ref)
    m_ref[...] = jnp.full(m_ref.shape, NEG_BIG, jnp.float32)
    l_ref[...] = jnp.zeros_like(l_ref)


def _diff_kernel(qa_ref, qb_ref, k_ref, v_ref, lam_ref, gain_ref, oa_ref, ob_ref,
                 vt_ref, qs_ref, acc_ref, m_ref, l_ref,
                 s0_ref, s1_ref, p0_ref, p1_ref, a0_ref, a1_ref, *, t, lam_init, seq):
    qi = pl.program_id(2)
    nq = seq // t
    s_bufs, p_bufs, a_bufs = (s0_ref, s1_ref), (p0_ref, p1_ref), (a0_ref, a1_ref)

    @pl.when(qi == 0)
    def _():
        _build_vt(v_ref, vt_ref, seq)

    qs_ref[0] = _stack_heads(qa_ref[0], SCALE * LOG2E)
    qs_ref[1] = _stack_heads(qb_ref[0], SCALE * LOG2E)
    _init_pair_state(acc_ref, m_ref, l_ref)
    where = _pair_steps(qi, nq)

    def logits(step, slot):
        tile, blk = where(step)
        s_bufs[slot][...] = _dot_nt(k_ref[0, pl.ds(pl.multiple_of(blk * t, t), t), :], qs_ref[tile])

    def softmax(step, slot, masked):
        tile, _ = where(step)
        s = s_bufs[slot][...]
        if masked:
            s = jnp.where(_stacked_causal(t, False), s, -jnp.inf)
        _softmax_stage(s, m_ref.at[tile], l_ref.at[tile], p_bufs[slot], a_bufs[slot])

    def pv(step, slot):
        tile, blk = where(step)
        vt = vt_ref[:, pl.ds(pl.multiple_of(blk * t, t), t)]
        acc_ref[tile] = a_bufs[slot][...] * acc_ref[tile] + _dot(vt, p_bufs[slot][...])

    _pipeline_pair(nq + 1, logits, softmax, pv)

    lp = lam_ref[...]
    lam = (jnp.exp(jnp.sum(lp[0:1] * lp[1:2], axis=-1, keepdims=True))
           - jnp.exp(jnp.sum(lp[2:3] * lp[3:4], axis=-1, keepdims=True)) + lam_init)
    for tile, o_ref in enumerate((oa_ref, ob_ref)):
        yt = acc_ref[tile] / l_ref[tile]
        y = (yt[:, :t] - lam * yt[:, t:]).T
        y = _rms(y, gain_ref[...], SUBLN_EPS) * (1.0 - lam_init)
        o_ref[0] = y.astype(o_ref.dtype)


def _pair_out(b, s, width):
    return [jax.ShapeDtypeStruct((b, s // 2, width), jnp.bfloat16)] * 2


def _join_halves(lo, hi):
    return jnp.concatenate([lo, hi], axis=1)


def _diff_attention(p3, lam_rows, gain, layer_idx, *, t=512):
    b, s, _ = p3.shape
    t = min(t, s // 2)
    nq = s // t
    lam_init = 0.8 - 0.6 * math.exp(-0.3 * layer_idx)
    nh = 4
    return _join_halves(*pl.pallas_call(
        functools.partial(_diff_kernel, t=t, lam_init=lam_init, seq=s),
        grid=(b, nh, nq // 2),
        in_specs=[
            pl.BlockSpec((1, t, LANES), lambda bi, h, qi: (bi, qi, BLK_AQ + h)),
            pl.BlockSpec((1, t, LANES), lambda bi, h, qi: (bi, nq - 1 - qi, BLK_AQ + h)),
            pl.BlockSpec((1, s, LANES), lambda bi, h, qi: (bi, 0, BLK_AK + h)),
            pl.BlockSpec((1, s, LANES), lambda bi, h, qi: (bi, 0, BLK_AV + h)),
            pl.BlockSpec((8, LANES), lambda bi, h, qi: (0, 0)),
            pl.BlockSpec((1, LANES), lambda bi, h, qi: (0, 0)),
        ],
        out_specs=[pl.BlockSpec((1, t, LANES), lambda bi, h, qi: (bi, qi, h)),
                   pl.BlockSpec((1, t, LANES), lambda bi, h, qi: (bi, nq // 2 - 1 - qi, h))],
        out_shape=_pair_out(b, s, nh * LANES),
        scratch_shapes=_flash_t_scratch(s, t),
        compiler_params=_cparams(("arbitrary", "arbitrary", "arbitrary")),
        name="diff_attn",
    )(p3, p3, p3, p3, lam_rows, gain))


def _stack_heads(q, scale):
    lo, hi = _half_masks(jnp.float32)
    q = q.astype(jnp.float32) * scale
    return jnp.concatenate([(q * lo).astype(jnp.bfloat16), (q * hi).astype(jnp.bfloat16)], axis=0)


def _stacked_causal(t, strict):
    kr = lax.broadcasted_iota(jnp.int32, (t, 2 * t), 0)
    qc = lax.broadcasted_iota(jnp.int32, (t, 2 * t), 1)
    qc = jnp.where(qc >= t, qc - t, qc)
    return (kr < qc) if strict else (kr <= qc)


def _unstack_heads_t(acc, t):
    return jnp.concatenate([acc[:HEAD_DIM, :t], acc[HEAD_DIM:, t:]], axis=0).T


def _fox_kernel(qa_ref, qb_ref, k_ref, v_ref, cr_ref, ct_ref, oa_ref, ob_ref,
                vt_ref, qs_ref, acc_ref, m_ref, l_ref,
                s0_ref, s1_ref, p0_ref, p1_ref, a0_ref, a1_ref, cq_ref, *, t, seq):
    pair = pl.program_id(1)
    qi = pl.program_id(2)
    nq = seq // t
    s_bufs, p_bufs, a_bufs = (s0_ref, s1_ref), (p0_ref, p1_ref), (a0_ref, a1_ref)

    @pl.when(qi == 0)
    def _():
        _build_vt(v_ref, vt_ref, seq)

    for tile, (q_ref, row) in enumerate(((qa_ref, qi), (qb_ref, nq - 1 - qi))):
        qs_ref[tile] = _stack_heads(q_ref[0], SCALE * LOG2E)
        qoff = pl.multiple_of(row * t, t)
        cq_ref[tile] = jnp.concatenate(
            [ct_ref[0, pl.ds(2 * pair + i, 1), pl.ds(qoff, t)] for i in range(2)], axis=1)
    _init_pair_state(acc_ref, m_ref, l_ref)
    where = _pair_steps(qi, nq)

    def logits(step, slot):
        tile, blk = where(step)
        s_bufs[slot][...] = _dot_nt(k_ref[0, pl.ds(pl.multiple_of(blk * t, t), t), :], qs_ref[tile])

    def softmax(step, slot, masked):
        tile, blk = where(step)
        off = pl.multiple_of(blk * t, t)
        ck = jnp.concatenate([jnp.tile(cr_ref[0, i, pl.ds(off, t), :], (1, t // LANES)) for i in range(2)],
                             axis=1)
        s = (s_bufs[slot][...] + cq_ref[tile]) - ck
        if masked:
            s = jnp.where(_stacked_causal(t, False), s, -jnp.inf)
        _softmax_stage(s, m_ref.at[tile], l_ref.at[tile], p_bufs[slot], a_bufs[slot])

    def pv(step, slot):
        tile, blk = where(step)
        vt = vt_ref[:, pl.ds(pl.multiple_of(blk * t, t), t)]
        acc_ref[tile] = a_bufs[slot][...] * acc_ref[tile] + _dot(vt, p_bufs[slot][...])

    _pipeline_pair(nq + 1, logits, softmax, pv)

    for tile, o_ref in enumerate((oa_ref, ob_ref)):
        o_ref[0] = _unstack_heads_t(acc_ref[tile] / l_ref[tile], t).astype(o_ref.dtype)


def _flash_t_scratch(s, t):
    row = pltpu.VMEM((1, 2 * t), jnp.float32)
    rows = pltpu.VMEM((2, 1, 2 * t), jnp.float32)
    return [pltpu.VMEM((LANES, s), jnp.bfloat16),
            pltpu.VMEM((2, 2 * t, LANES), jnp.bfloat16),
            pltpu.VMEM((2, LANES, 2 * t), jnp.float32),
            rows, rows,
            pltpu.VMEM((t, 2 * t), jnp.float32), pltpu.VMEM((t, 2 * t), jnp.float32),
            pltpu.VMEM((t, 2 * t), jnp.bfloat16), pltpu.VMEM((t, 2 * t), jnp.bfloat16),
            row, row]


def _fox_attention(p3, cum_rep, cum_t, *, t=256):
    b, s, _ = p3.shape
    t = min(t, s // 2)
    nq = s // t
    return _join_halves(*pl.pallas_call(
        functools.partial(_fox_kernel, t=t, seq=s),
        grid=(b, 2, nq // 2),
        in_specs=[
            pl.BlockSpec((1, t, LANES), lambda bi, p, qi: (bi, qi, BLK_BQ + p)),
            pl.BlockSpec((1, t, LANES), lambda bi, p, qi: (bi, nq - 1 - qi, BLK_BQ + p)),
            pl.BlockSpec((1, s, LANES), lambda bi, p, qi: (bi, 0, BLK_BK + p)),
            pl.BlockSpec((1, s, LANES), lambda bi, p, qi: (bi, 0, BLK_BV + p)),
            pl.BlockSpec((1, 2, s, LANES), lambda bi, p, qi: (bi, p, 0, 0)),
            pl.BlockSpec((1, 8, s), lambda bi, p, qi: (bi, 0, 0)),
        ],
        out_specs=[pl.BlockSpec((1, t, LANES), lambda bi, p, qi: (bi, qi, p)),
                   pl.BlockSpec((1, t, LANES), lambda bi, p, qi: (bi, nq // 2 - 1 - qi, p))],
        out_shape=_pair_out(b, s, 2 * LANES),
        scratch_shapes=_flash_t_scratch(s, t) + [pltpu.VMEM((2, 1, 2 * t), jnp.float32)],
        compiler_params=_cparams(("arbitrary", "arbitrary", "arbitrary")),
        name="fox_attn",
    )(p3, p3, p3, p3, cum_rep, cum_t))


def _sb_kernel(qa_ref, qb_ref, k_ref, v_ref, oa_ref, ob_ref,
               vt_ref, qs_ref, acc_ref, run_ref,
               s0_ref, s1_ref, p0_ref, p1_ref, *, t, seq):
    qi = pl.program_id(2)
    nq = seq // t
    s_bufs, p_bufs = (s0_ref, s1_ref), (p0_ref, p1_ref)

    @pl.when(qi == 0)
    def _():
        _build_vt(v_ref, vt_ref, seq)

    qs_ref[0] = _stack_heads(qa_ref[0], SCALE * LOG2E)
    qs_ref[1] = _stack_heads(qb_ref[0], SCALE * LOG2E)
    r = lax.broadcasted_iota(jnp.int32, (t, 2 * t), 0)
    c = lax.broadcasted_iota(jnp.int32, (t, 2 * t), 1)
    later2 = (jnp.where(c >= t, c - t, c) > r).astype(jnp.bfloat16)
    acc_ref[...] = jnp.zeros_like(acc_ref)
    run_ref[...] = jnp.zeros_like(run_ref)
    where = _pair_steps(qi, nq, reverse=True)

    def logits(step, slot):
        tile, blk = where(step)
        s_bufs[slot][...] = _dot_nt(k_ref[0, pl.ds(pl.multiple_of(blk * t, t), t), :], qs_ref[tile])

    def weights(step, slot, masked):
        tile, _ = where(step)
        z = s_bufs[slot][...]
        log_beta = jnp.minimum(z, 0.0) - jnp.log2(1.0 + jnp.exp2(-jnp.abs(z)))
        l1m = log_beta - z
        if masked:
            cm = _stacked_causal(t, True)
            l1m = jnp.where(cm, l1m, 0.0)
        l_hi = l1m.astype(jnp.bfloat16)
        l_lo = (l1m - l_hi.astype(jnp.float32)).astype(jnp.bfloat16)
        after = _dot(later2, jnp.concatenate([l_hi, l_lo], axis=0)) + run_ref[tile]
        a = jnp.exp2(log_beta + after)
        if masked:
            a = jnp.where(cm, a, 0.0)
        p_bufs[slot][...] = a.astype(jnp.bfloat16)
        run_ref[tile] += jnp.sum(l1m, axis=0, keepdims=True)

    def pv(step, slot):
        tile, blk = where(step)
        acc_ref[tile] += _dot(vt_ref[:, pl.ds(pl.multiple_of(blk * t, t), t)], p_bufs[slot][...])

    _pipeline_pair(nq + 1, logits, weights, pv)

    for tile, o_ref in enumerate((oa_ref, ob_ref)):
        o_ref[0] = _unstack_heads_t(acc_ref[tile], t).astype(o_ref.dtype)


def _sb_attention(p3, *, t=256):
    b, s, _ = p3.shape
    t = min(t, s // 2)
    nq = s // t
    return _join_halves(*pl.pallas_call(
        functools.partial(_sb_kernel, t=t, seq=s),
        grid=(b, 2, nq // 2),
        in_specs=[
            pl.BlockSpec((1, t, LANES), lambda bi, p, qi: (bi, qi, BLK_CQ + p)),
            pl.BlockSpec((1, t, LANES), lambda bi, p, qi: (bi, nq - 1 - qi, BLK_CQ + p)),
            pl.BlockSpec((1, s, LANES), lambda bi, p, qi: (bi, 0, BLK_CK + p)),
            pl.BlockSpec((1, s, LANES), lambda bi, p, qi: (bi, 0, BLK_CV + p)),
        ],
        out_specs=[pl.BlockSpec((1, t, LANES), lambda bi, p, qi: (bi, qi, p)),
                   pl.BlockSpec((1, t, LANES), lambda bi, p, qi: (bi, nq // 2 - 1 - qi, p))],
        out_shape=_pair_out(b, s, 2 * LANES),
        scratch_shapes=[pltpu.VMEM((LANES, s), jnp.bfloat16),
                        pltpu.VMEM((2, 2 * t, LANES), jnp.bfloat16),
                        pltpu.VMEM((2, LANES, 2 * t), jnp.float32),
                        pltpu.VMEM((2, 1, 2 * t), jnp.float32),
                        pltpu.VMEM((t, 2 * t), jnp.float32), pltpu.VMEM((t, 2 * t), jnp.float32),
                        pltpu.VMEM((t, 2 * t), jnp.bfloat16), pltpu.VMEM((t, 2 * t), jnp.bfloat16)],
        compiler_params=_cparams(("arbitrary", "arbitrary", "arbitrary")),
        name="sb_attn",
    )(p3, p3, p3, p3))


def _dsa_kernel(iq_ref, q_ref, sm_ref, ik_ref, k_ref, v_ref, o_ref,
                keys_ref, bias_ref, vt_ref, sig_ref, acc_ref, m_ref, l_ref,
                s0_ref, s1_ref, *, tq, ck, topk, seq):
    qi = pl.program_id(1)
    nblk = (qi * tq + tq + ck - 1) // ck
    lo, hi = _half_masks(jnp.float32)
    halves = (lo, hi)

    @pl.when(qi == 0)
    def _():
        _build_vt(v_ref, vt_ref, seq)

    iw_t = sm_ref[0].T[SM_IW:SM_IW + N_IDX_HEADS, :]
    iq = iq_ref[0].astype(jnp.float32)
    iq_stack = jnp.concatenate(
        [(iq[:, (h // 2) * LANES:(h // 2 + 1) * LANES] * halves[h % 2]).astype(jnp.bfloat16)
         for h in range(N_IDX_HEADS)], axis=0)
    qds = [_stack_heads(q_ref[0][:, pr * LANES:(pr + 1) * LANES], SCALE * LOG2E) for pr in range(2)]
    t_idx = qi * tq + lax.broadcasted_iota(jnp.int32, (1, tq), 1)
    row = lax.broadcasted_iota(jnp.int32, (ck, tq), 0)

    def score_chunk(c, _):
        off = pl.multiple_of(c * ck, ck)
        ikc = ik_ref[0, pl.ds(off, ck), :]
        rel = _dot_nt(ikc, iq_stack)
        acc = jnp.zeros((ck, tq), jnp.float32)
        for h in range(N_IDX_HEADS):
            acc = acc + iw_t[h:h + 1, :] * jnp.maximum(rel[:, h * tq:(h + 1) * tq], 0.0)
        bits = pltpu.bitcast(acc, jnp.int32)
        key = bits ^ ((bits >> 31) & 0x7FFFFFFF)
        keys_ref[pl.ds(off, ck), :] = jnp.where(row + off <= t_idx, key, INT_MIN)
        return 0

    lax.fori_loop(0, nblk, score_chunk, 0)

    def count(pred_fn):
        def body(c, cnt):
            off = pl.multiple_of(c * ck, ck)
            hit = pred_fn(keys_ref[pl.ds(off, ck), :], off).astype(jnp.int32)
            lanes = 4
            accs = [hit[g * 8:(g + 1) * 8] for g in range(lanes)]
            for g in range(lanes, ck // 8):
                accs[g % lanes] = accs[g % lanes] + hit[g * 8:(g + 1) * 8]
            return cnt + ((accs[0] + accs[1]) + (accs[2] + accs[3]))
        cnt8 = lax.fori_loop(0, nblk, body, jnp.zeros((8, tq), jnp.int32))
        return jnp.sum(cnt8, axis=0, keepdims=True)

    c0 = count(lambda kk, off: kk >= 0)
    tau0 = jnp.where(c0 >= topk, 0, INT_MIN).astype(jnp.int32)
    cnt0 = jnp.where(c0 >= topk, c0, nblk * ck).astype(jnp.int32)

    def bit_step(i, carry):
        tau, cnt_ge = carry
        cand = tau + jnp.left_shift(jnp.int32(1), 30 - i)
        cnt = count(lambda kk, off: kk >= cand)
        ok = cnt >= topk
        return jnp.where(ok, cand, tau), jnp.where(ok, cnt, cnt_ge)

    tau, cnt_ge = lax.fori_loop(0, 31, bit_step, (tau0, cnt0))
    real = tau != INT_MIN
    sig_ref[...] = jnp.where(real, seq, -1).astype(jnp.int32)
    over = jnp.max(jnp.where(real & (cnt_ge > topk), 1, 0))

    @pl.when(over > 0)
    def _():
        cnt_gt = count(lambda kk, off: kk > tau)
        need = topk - cnt_gt

        def idx_step(i, x):
            cand = x + jnp.left_shift(jnp.int32(1), (seq.bit_length() - 2) - i)
            f = count(lambda kk, off: (kk == tau) & (row + off < cand))
            return jnp.where(f < need, cand, x)

        x = lax.fori_loop(0, seq.bit_length() - 1, idx_step, jnp.zeros((1, tq), jnp.int32))
        sig_ref[...] = jnp.where(real, x, -1)

    sigma = sig_ref[...]

    def bias_chunk(c, _):
        off = pl.multiple_of(c * ck, ck)
        kk = keys_ref[pl.ds(off, ck), :]
        sel = (kk > tau) | ((kk == tau) & (row + off <= sigma))
        bias_ref[pl.ds(off, ck), :] = jnp.where(sel, 0.0, -jnp.inf)
        return 0

    lax.fori_loop(0, nblk, bias_chunk, 0)

    s_bufs = (s0_ref, s1_ref)
    _init_pair_state(acc_ref, m_ref, l_ref)

    def logits(c, slot):
        off = pl.multiple_of(c * ck, ck)
        for pr in range(2):
            s_bufs[slot][:, 2 * pr * tq:2 * (pr + 1) * tq] = _dot_nt(
                k_ref[0, pl.ds(off, ck), pr * LANES:(pr + 1) * LANES], qds[pr])

    def consume(c, slot):
        off = pl.multiple_of(c * ck, ck)
        s = s_bufs[slot][...] + jnp.tile(bias_ref[pl.ds(off, ck), :], (1, 4))
        m = m_ref[...]
        m_new = jnp.maximum(m, jnp.max(s, axis=0, keepdims=True))
        alpha = jnp.exp2(m - m_new)
        p = jnp.exp2(s - m_new)
        l_ref[...] = alpha * l_ref[...] + jnp.sum(p, axis=0, keepdims=True)
        m_ref[...] = m_new
        p = p.astype(jnp.bfloat16)
        upd = jnp.concatenate(
            [_dot(vt_ref[h * HEAD_DIM:(h + 1) * HEAD_DIM, pl.ds(off, ck)], p[:, h * tq:(h + 1) * tq])
             for h in range(4)], axis=1)
        acc_ref[...] = alpha * acc_ref[...] + upd

    def pair_step(i, _):
        c = 2 * i
        logits(c + 1, 1)
        consume(c, 0)
        logits(jnp.minimum(c + 2, nblk - 1), 0)
        consume(c + 1, 1)
        return 0

    logits(0, 0)
    lax.fori_loop(0, nblk // 2, pair_step, 0)

    @pl.when(nblk % 2 == 1)
    def _():
        consume(nblk - 1, 0)

    out = acc_ref[...] / l_ref[...]
    out_t = jnp.concatenate([out[:, h * tq:(h + 1) * tq] for h in range(4)], axis=0)
    o_ref[0] = out_t.T.astype(o_ref.dtype)


def _dsa_attention(p3, small3, *, tq=256, ck=256):
    b, s, _ = p3.shape
    ck = min(ck, s)
    topk = min(TOPK_MAX, s // 4)
    return pl.pallas_call(
        functools.partial(_dsa_kernel, tq=tq, ck=ck, topk=topk, seq=s),
        grid=(b, s // tq),
        in_specs=[
            pl.BlockSpec((1, tq, 4 * LANES), lambda bi, qi: (bi, qi, BLK_IQ // 4)),
            pl.BlockSpec((1, tq, 2 * LANES), lambda bi, qi: (bi, qi, BLK_DQ // 2)),
            pl.BlockSpec((1, tq, LANES), lambda bi, qi: (bi, qi, 0)),
            pl.BlockSpec((1, s, LANES), lambda bi, qi: (bi, 0, BLK_IK)),
            pl.BlockSpec((1, s, 2 * LANES), lambda bi, qi: (bi, 0, BLK_DK // 2)),
            pl.BlockSpec((1, s, 2 * LANES), lambda bi, qi: (bi, 0, BLK_DV // 2)),
        ],
        out_specs=pl.BlockSpec((1, tq, 2 * LANES), lambda bi, qi: (bi, qi, 0)),
        out_shape=jax.ShapeDtypeStruct((b, s, 2 * LANES), jnp.bfloat16),
        scratch_shapes=[
            pltpu.VMEM((s, tq), jnp.int32),
            pltpu.VMEM((s, tq), jnp.float32),
            pltpu.VMEM((4 * HEAD_DIM, s), jnp.bfloat16),
            pltpu.VMEM((1, tq), jnp.int32),
            pltpu.VMEM((HEAD_DIM, 4 * tq), jnp.float32),
            pltpu.VMEM((1, 4 * tq), jnp.float32), pltpu.VMEM((1, 4 * tq), jnp.float32),
            pltpu.VMEM((ck, 4 * tq), jnp.float32), pltpu.VMEM((ck, 4 * tq), jnp.float32),
        ],
        compiler_params=_cparams(("arbitrary", "arbitrary")),
        name="dsa_attn",
    )(p3, p3, small3, p3, p3, p3)


def _merge_kernel(x_ref, g_ref, wgate_ref, bgate_ref, ya_ref, yb_ref, yc_ref, yd_ref,
                  wa_ref, wb_ref, wc_ref, wd_ref, wo_ref, o_ref):
    x = x_ref[...]
    d = x.shape[1]
    hb = _rms(x, g_ref[...], NORM_EPS).astype(jnp.bfloat16)
    merged = jnp.zeros(x.shape, jnp.float32)
    for i, (y_ref, w_ref) in enumerate(((ya_ref, wa_ref), (yb_ref, wb_ref), (yc_ref, wc_ref), (yd_ref, wd_ref))):
        gate = jax.nn.sigmoid(_dot(hb, wgate_ref[:, i * d:(i + 1) * d]) + bgate_ref[:, i * d:(i + 1) * d])
        merged = merged + gate * _dot(y_ref[...], w_ref[...])
    o_ref[...] = x + _dot(merged.astype(jnp.bfloat16), wo_ref[...])


def _merge(x2, gain, wgate, bgate, ya, yb, yc, yd, wa, wb, wc, wd, wo, *, tm=512):
    m, d = x2.shape
    rowblk = lambda w: pl.BlockSpec((tm, w), lambda i: (i, 0))
    return pl.pallas_call(
        _merge_kernel,
        grid=(m // tm,),
        in_specs=[
            rowblk(d), _resident((1, d)), _resident((d, N_BRANCH * d)), _resident((1, N_BRANCH * d)),
            rowblk(ya.shape[1]), rowblk(yb.shape[1]), rowblk(yc.shape[1]), rowblk(yd.shape[1]),
            _resident(wa.shape), _resident(wb.shape), _resident(wc.shape), _resident(wd.shape),
            _resident(wo.shape),
        ],
        out_specs=rowblk(d),
        out_shape=jax.ShapeDtypeStruct((m, d), jnp.float32),
        compiler_params=_cparams(("parallel",)),
        name="merge",
    )(x2, gain, wgate, bgate, ya, yb, yc, yd, wa, wb, wc, wd, wo)


def _proj_weights(w_in, b_fgt):
    d = w_in.shape[0]
    o = 0

    def take(width):
        nonlocal o
        seg = w_in[:, o:o + width]
        o += width
        return seg

    aq, ak, av = take(512), take(512), take(512)
    bq, bk, bv, bf = take(256), take(256), take(256), take(4)
    cq, ck, cv = take(256), take(256), take(256)
    dq, dk, dv = take(256), take(256), take(256)
    diq, dik, diw = take(512), take(64), take(8)
    zeros = jnp.zeros((d, LANES), w_in.dtype)
    big = jnp.concatenate([aq, ak, av, bq, bk, bv, cq, ck, cv, dq, dk, dv, dik, dik, zeros, diq], axis=1)
    small = jnp.concatenate([diw, bf, jnp.zeros((d, LANES - 12), w_in.dtype)], axis=1)
    bias = jnp.concatenate([jnp.zeros((8,), jnp.float32), b_fgt.astype(jnp.float32),
                            jnp.zeros((LANES - 12,), jnp.float32)]).reshape(1, LANES)
    return big.astype(jnp.bfloat16), small.astype(jnp.bfloat16), bias


def kernel(x, positions, ffn1_norm, ffn1_w_gu, ffn1_w_down, mix_norm, w_in, b_fgt, lam_q1, lam_k1, lam_q2, lam_k2, diff_gain, w_gate, b_gate, w_br_a, w_br_b, w_br_c, w_br_d, w_out, ffn2_norm, ffn2_w_gu, ffn2_w_down, final_norm):
    b, s, d = x.shape
    depth = w_in.shape[0]
    bf = jnp.bfloat16
    cos_t, sin1_t, sin2_t = _rope_tables(positions)
    fgain = final_norm.reshape(1, d)
    x2 = x.reshape(b * s, d)
    for l in range(depth):
        x2 = _ffn(x2, ffn1_norm[l].reshape(1, d), ffn1_w_gu[l][:, :D_FF].astype(bf),
                  ffn1_w_gu[l][:, D_FF:].astype(bf), ffn1_w_down[l].astype(bf), fgain, final_norm=False)
        wbig, wsmall, bsmall = _proj_weights(w_in[l], b_fgt[l])
        p2, small2 = _proj(x2, mix_norm[l].reshape(1, d), wbig, wsmall, bsmall, cos_t, sin1_t, sin2_t, s)
        p3 = p2.reshape(b, s, PROJ_W)
        small3 = small2.reshape(b, s, LANES)
        cum_n, cum_t = _cumsum(small3)
        lam_rows = jnp.zeros((8, LANES), jnp.float32)
        lam_rows = lam_rows.at[0:4, 0:HEAD_DIM].set(jnp.stack([lam_q1[l], lam_k1[l], lam_q2[l], lam_k2[l]]))
        ya = _diff_attention(p3, lam_rows, diff_gain[l].reshape(1, LANES), l)
        yb = _fox_attention(p3, cum_n, cum_t)
        yc = _sb_attention(p3)
        yd = _dsa_attention(p3, small3)
        x2 = _merge(x2, mix_norm[l].reshape(1, d), w_gate[l].astype(bf), b_gate[l].reshape(1, N_BRANCH * d),
                    ya.reshape(b * s, -1), yb.reshape(b * s, -1), yc.reshape(b * s, -1), yd.reshape(b * s, -1),
                    w_br_a[l].astype(bf), w_br_b[l].astype(bf), w_br_c[l].astype(bf), w_br_d[l].astype(bf),
                    w_out[l].astype(bf))
        x2 = _ffn(x2, ffn2_norm[l].reshape(1, d), ffn2_w_gu[l][:, :D_FF].astype(bf),
                  ffn2_w_gu[l][:, D_FF:].astype(bf), ffn2_w_down[l].astype(bf), fgain,
                  final_norm=(l == depth - 1))
    return x2.reshape(b, s, d)
```

```python
import functools
import math

import numpy as np
import jax
import jax.numpy as jnp
from jax import lax
from jax.experimental import pallas as pl
from jax.experimental.pallas import tpu as pltpu

D_MODEL = 1024
HEAD_DIM = 64
N_IDX_HEADS = 8
TOPK_MAX = 256
ROPE_THETA = 500000.0
ROPE_DIM = HEAD_DIM // 4
D_FF = 2816
N_BRANCH = 4
NORM_EPS = 1e-6
SUBLN_EPS = 1e-5
SCALE = HEAD_DIM ** -0.5
LOG2E = math.log2(math.e)

LANES = 128
VMEM_LIMIT = 56 * 1024 * 1024

BLK_AQ, BLK_AK, BLK_AV = 0, 4, 8
BLK_BQ, BLK_BK, BLK_BV = 12, 14, 16
BLK_CQ, BLK_CK, BLK_CV = 18, 20, 22
BLK_DQ, BLK_DK, BLK_DV = 24, 26, 28
BLK_IK, BLK_SPARE, BLK_IQ = 30, 31, 32
N_PROJ_BLKS = 36
PROJ_W = N_PROJ_BLKS * LANES
ROPE_BLKS = frozenset(list(range(0, 8)) + [24, 25, 26, 27, 30] + list(range(32, 36)))
SM_IW, SM_F = 0, 8

NEG_BIG = -1e30
INT_MIN = -(2 ** 31)


def _cparams(sem, vmem=VMEM_LIMIT):
    return pltpu.CompilerParams(dimension_semantics=sem, vmem_limit_bytes=vmem)


def _resident(shape):
    nd = len(shape)
    return pl.BlockSpec(shape, lambda *_: (0,) * nd, pipeline_mode=pl.Buffered(1))


def _rms(x, g, eps):
    ms = jnp.mean(x * x, axis=-1, keepdims=True)
    return x * lax.rsqrt(ms + eps) * g


def _dot(a, b):
    return jnp.dot(a, b, preferred_element_type=jnp.float32)


def _dot_nt(a, b):
    return lax.dot_general(a, b, (((1,), (1,)), ((), ())), preferred_element_type=jnp.float32)


def _ffn_kernel(x_ref, g_ref, wg_ref, wu_ref, wd_ref, fg_ref, o_ref, *, tf, final_norm):
    x = x_ref[...]
    hb = _rms(x, g_ref[...], NORM_EPS).astype(jnp.bfloat16)
    acc = jnp.zeros(x.shape, jnp.float32)
    for j in range(D_FF // tf):
        sl = slice(j * tf, (j + 1) * tf)
        g = _dot(hb, wg_ref[:, sl])
        u = _dot(hb, wu_ref[:, sl])
        a = (g * jax.nn.sigmoid(g) * u).astype(jnp.bfloat16)
        acc = acc + _dot(a, wd_ref[sl, :])
    y = x + 0.5 * acc
    if final_norm:
        y = _rms(y, fg_ref[...], NORM_EPS)
    o_ref[...] = y


def _ffn(x2, gain, wg, wu, wd, fgain, *, final_norm, tm=512, tf=256):
    m, d = x2.shape
    return pl.pallas_call(
        functools.partial(_ffn_kernel, tf=tf, final_norm=final_norm),
        grid=(m // tm,),
        in_specs=[
            pl.BlockSpec((tm, d), lambda i: (i, 0)),
            _resident((1, d)),
            _resident((d, D_FF)),
            _resident((d, D_FF)),
            _resident((D_FF, d)),
            _resident((1, d)),
        ],
        out_specs=pl.BlockSpec((tm, d), lambda i: (i, 0)),
        out_shape=jax.ShapeDtypeStruct((m, d), jnp.float32),
        compiler_params=_cparams(("parallel",)),
        name="ffn",
    )(x2, gain, wg, wu, wd, fgain)


def _rope_kernel(pos_ref, freq_ref, m1_ref, m2_ref, c_ref, s1_ref, s2_ref):
    ang = pos_ref[...].astype(jnp.float32) * freq_ref[...]
    c = jnp.cos(ang)
    s = jnp.sin(ang)
    c_ref[...] = c
    s1_ref[...] = s * m1_ref[...]
    s2_ref[...] = s * m2_ref[...]


def _rope_tables(positions):
    s = positions.shape[0]
    freqs = ROPE_THETA ** (-jnp.arange(0, ROPE_DIM, 2, dtype=jnp.float32) / ROPE_DIM)
    half = ROPE_DIM // 2
    d = np.arange(LANES) % HEAD_DIM
    freq_row = jnp.where(d < ROPE_DIM, freqs[d % half], 0.0).reshape(1, LANES)
    m1 = jnp.asarray(np.where(d < half, -1.0, 0.0).astype(np.float32)).reshape(1, LANES)
    m2 = jnp.asarray(np.where((d >= half) & (d < ROPE_DIM), 1.0, 0.0).astype(np.float32)).reshape(1, LANES)
    tab = jax.ShapeDtypeStruct((s, LANES), jnp.float32)
    ts = min(s, 1024)
    row = pl.BlockSpec((1, LANES), lambda i: (0, 0))
    blk = pl.BlockSpec((ts, LANES), lambda i: (i, 0))
    return pl.pallas_call(
        _rope_kernel,
        grid=(s // ts,),
        in_specs=[pl.BlockSpec((ts, 1), lambda i: (i, 0)), row, row, row],
        out_specs=[blk, blk, blk],
        out_shape=[tab, tab, tab],
        compiler_params=_cparams(("parallel",)),
        name="rope_tables",
    )(positions.reshape(s, 1), freq_row, m1, m2)


def _proj_kernel(x_ref, g_ref, w_ref, wsm_ref, bsm_ref, c_ref, s1_ref, s2_ref, p_ref, sm_ref):
    hb = _rms(x_ref[...], g_ref[...], NORM_EPS).astype(jnp.bfloat16)
    c, s1, s2 = c_ref[...], s1_ref[...], s2_ref[...]
    chunk = 4
    for cb in range(N_PROJ_BLKS // chunk):
        y = _dot(hb, w_ref[:, cb * chunk * LANES:(cb + 1) * chunk * LANES])
        for sub in range(chunk):
            blk = cb * chunk + sub
            ys = y[:, sub * LANES:(sub + 1) * LANES]
            if blk in ROPE_BLKS:
                ys = (ys * c + pltpu.roll(ys, LANES - ROPE_DIM // 2, 1) * s1
                      + pltpu.roll(ys, ROPE_DIM // 2, 1) * s2)
            p_ref[:, blk * LANES:(blk + 1) * LANES] = ys.astype(jnp.bfloat16)
    sm_ref[...] = _dot(hb, wsm_ref[...]) + bsm_ref[...]


def _proj(x2, gain, w, wsm, bsm, c, s1, s2, seq, *, tm=512):
    m, d = x2.shape
    tm = min(tm, seq)
    nrep = seq // tm
    tab = pl.BlockSpec((tm, LANES), lambda i: (i % nrep, 0))
    return pl.pallas_call(
        _proj_kernel,
        grid=(m // tm,),
        in_specs=[
            pl.BlockSpec((tm, d), lambda i: (i, 0)),
            _resident((1, d)),
            _resident((d, PROJ_W)),
            _resident((d, LANES)),
            _resident((1, LANES)),
            tab, tab, tab,
        ],
        out_specs=[pl.BlockSpec((tm, PROJ_W), lambda i: (i, 0)),
                   pl.BlockSpec((tm, LANES), lambda i: (i, 0))],
        out_shape=[jax.ShapeDtypeStruct((m, PROJ_W), jnp.bfloat16),
                   jax.ShapeDtypeStruct((m, LANES), jnp.float32)],
        compiler_params=_cparams(("parallel",)),
        name="proj",
    )(x2, gain, w, wsm, bsm, c, s1, s2)


def _cumsum_kernel(sm_ref, cn_ref, ct_ref):
    t = sm_ref[0].T
    n = t.shape[1]
    ls = jnp.minimum(t, 0.0) - jnp.log(1.0 + jnp.exp(-jnp.abs(t)))
    lane = lax.broadcasted_iota(jnp.int32, ls.shape, 1)
    sh = 1
    while sh < n:
        ls = ls + jnp.where(lane >= sh, pltpu.roll(ls, sh, 1), 0.0)
        sh *= 2
    ls = ls * LOG2E
    ct_ref[0] = ls[SM_F:SM_F + 8, :]
    for h in range(4):
        cn_ref[0, h] = jnp.broadcast_to(ls[SM_F + h:SM_F + h + 1, :], ls.shape).T


def _cumsum(small3):
    b, s, _ = small3.shape
    return pl.pallas_call(
        _cumsum_kernel,
        grid=(b,),
        in_specs=[pl.BlockSpec((1, s, LANES), lambda i: (i, 0, 0))],
        out_specs=[pl.BlockSpec((1, 4, s, LANES), lambda i: (i, 0, 0, 0)),
                   pl.BlockSpec((1, 8, s), lambda i: (i, 0, 0))],
        out_shape=[jax.ShapeDtypeStruct((b, 4, s, LANES), jnp.float32),
                   jax.ShapeDtypeStruct((b, 8, s), jnp.float32)],
        compiler_params=_cparams(("parallel",)),
        name="forget_cumsum",
    )(small3)


def _half_masks(dtype):
    lane = lax.broadcasted_iota(jnp.int32, (1, LANES), 1)
    lo = (lane < HEAD_DIM).astype(dtype)
    return lo, (1 - lo).astype(dtype)


def _pipeline_pair(n_steps, logits, weights, pv):
    assert n_steps % 2 == 1 and n_steps >= 3
    logits(0, 0)
    logits(1, 1)
    weights(0, 0, True)
    logits(2, 0)
    weights(1, 1, True)
    pv(0, 0)

    def body(i, _):
        s = 2 * i
        logits(s + 1, 1)
        weights(s, 0, False)
        pv(s - 1, 1)
        logits(s + 2, 0)
        weights(s + 1, 1, False)
        pv(s, 0)
        return 0

    lax.fori_loop(1, n_steps // 2, body, 0)
    pv(n_steps - 2, 1)
    weights(n_steps - 1, 0, False)
    pv(n_steps - 1, 0)


def _pair_steps(qi, nq, reverse=False):
    def where(step):
        u = step - 2
        in_b = jnp.where(step < 2, step == 1, u >= qi)
        v = jnp.where(in_b, u - qi, u)
        n_other = jnp.where(in_b, nq - 1 - qi, qi)
        blk = (n_other - 1 - v) if reverse else v
        return in_b.astype(jnp.int32), jnp.where(step < 2, n_other, blk)

    return where


def _build_vt(v_ref, vt_ref, seq, chunk=512):
    chunk = min(chunk, seq)
    for c in range(seq // chunk):
        blk = v_ref[0, c * chunk:(c + 1) * chunk, :].astype(jnp.float32)
        vt_ref[:, c * chunk:(c + 1) * chunk] = blk.T.astype(vt_ref.dtype)


def _softmax_stage(s, m_ref, l_ref, p_buf, a_buf):
    m = m_ref[...]
    m_new = jnp.maximum(m, jnp.max(s, axis=0, keepdims=True))
    alpha = jnp.exp2(m - m_new)
    p = jnp.exp2(s - m_new)
    l_ref[...] = alpha * l_ref[...] + jnp.sum(p, axis=0, keepdims=True)
    m_ref[...] = m_new
    a_buf[...] = alpha
    p_buf[...] = p.astype(jnp.bfloat16)


def _init_pair_state(acc_ref, m_ref, l_ref):
    acc_ref[...] = jnp.zeros_like(acc_ref)
    m_ref[...] = jnp.full(m_ref.shape, NEG_BIG, jnp.float32)
    l_ref[...] = jnp.zeros_like(l_ref)


def _diff_kernel(qa_ref, qb_ref, k_ref, v_ref, lam_ref, gain_ref, oa_ref, ob_ref,
                 vt_ref, qs_ref, acc_ref, m_ref, l_ref,
                 s0_ref, s1_ref, p0_ref, p1_ref, a0_ref, a1_ref, *, t, lam_init, seq):
    qi = pl.program_id(2)
    nq = seq // t
    s_bufs, p_bufs, a_bufs = (s0_ref, s1_ref), (p0_ref, p1_ref), (a0_ref, a1_ref)

    @pl.when(qi == 0)
    def _():
        _build_vt(v_ref, vt_ref, seq)

    qs_ref[0] = _stack_heads(qa_ref[0], SCALE * LOG2E)
    qs_ref[1] = _stack_heads(qb_ref[0], SCALE * LOG2E)
    _init_pair_state(acc_ref, m_ref, l_ref)
    where = _pair_steps(qi, nq)

    def logits(step, slot):
        tile, blk = where(step)
        s_bufs[slot][...] = _dot_nt(k_ref[0, pl.ds(pl.multiple_of(blk * t, t), t), :], qs_ref[tile])

    def softmax(step, slot, masked):
        tile, _ = where(step)
        s = s_bufs[slot][...]
        if masked:
            s = jnp.where(_stacked_causal(t, False), s, -jnp.inf)
        _softmax_stage(s, m_ref.at[tile], l_ref.at[tile], p_bufs[slot], a_bufs[slot])

    def pv(step, slot):
        tile, blk = where(step)
        vt = vt_ref[:, pl.ds(pl.multiple_of(blk * t, t), t)]
        acc_ref[tile] = a_bufs[slot][...] * acc_ref[tile] + _dot(vt, p_bufs[slot][...])

    _pipeline_pair(nq + 1, logits, softmax, pv)

    lp = lam_ref[...]
    lam = (jnp.exp(jnp.sum(lp[0:1] * lp[1:2], axis=-1, keepdims=True))
           - jnp.exp(jnp.sum(lp[2:3] * lp[3:4], axis=-1, keepdims=True)) + lam_init)
    for tile, o_ref in enumerate((oa_ref, ob_ref)):
        yt = acc_ref[tile] / l_ref[tile]
        y = (yt[:, :t] - lam * yt[:, t:]).T
        y = _rms(y, gain_ref[...], SUBLN_EPS) * (1.0 - lam_init)
        o_ref[0] = y.astype(o_ref.dtype)


def _pair_out(b, s, width):
    return [jax.ShapeDtypeStruct((b, s // 2, width), jnp.bfloat16)] * 2


def _join_halves(lo, hi):
    return jnp.concatenate([lo, hi], axis=1)


def _diff_attention(p3, lam_rows, gain, layer_idx, *, t=512):
    b, s, _ = p3.shape
    t = min(t, s // 2)
    nq = s // t
    lam_init = 0.8 - 0.6 * math.exp(-0.3 * layer_idx)
    nh = 4
    return _join_halves(*pl.pallas_call(
        functools.partial(_diff_kernel, t=t, lam_init=lam_init, seq=s),
        grid=(b, nh, nq // 2),
        in_specs=[
            pl.BlockSpec((1, t, LANES), lambda bi, h, qi: (bi, qi, BLK_AQ + h)),
            pl.BlockSpec((1, t, LANES), lambda bi, h, qi: (bi, nq - 1 - qi, BLK_AQ + h)),
            pl.BlockSpec((1, s, LANES), lambda bi, h, qi: (bi, 0, BLK_AK + h)),
            pl.BlockSpec((1, s, LANES), lambda bi, h, qi: (bi, 0, BLK_AV + h)),
            pl.BlockSpec((8, LANES), lambda bi, h, qi: (0, 0)),
            pl.BlockSpec((1, LANES), lambda bi, h, qi: (0, 0)),
        ],
        out_specs=[pl.BlockSpec((1, t, LANES), lambda bi, h, qi: (bi, qi, h)),
                   pl.BlockSpec((1, t, LANES), lambda bi, h, qi: (bi, nq // 2 - 1 - qi, h))],
        out_shape=_pair_out(b, s, nh * LANES),
        scratch_shapes=_flash_t_scratch(s, t),
        compiler_params=_cparams(("arbitrary", "arbitrary", "arbitrary")),
        name="diff_attn",
    )(p3, p3, p3, p3, lam_rows, gain))


def _stack_heads(q, scale):
    lo, hi = _half_masks(jnp.float32)
    q = q.astype(jnp.float32) * scale
    return jnp.concatenate([(q * lo).astype(jnp.bfloat16), (q * hi).astype(jnp.bfloat16)], axis=0)


def _stacked_causal(t, strict):
    kr = lax.broadcasted_iota(jnp.int32, (t, 2 * t), 0)
    qc = lax.broadcasted_iota(jnp.int32, (t, 2 * t), 1)
    qc = jnp.where(qc >= t, qc - t, qc)
    return (kr < qc) if strict else (kr <= qc)


def _unstack_heads_t(acc, t):
    return jnp.concatenate([acc[:HEAD_DIM, :t], acc[HEAD_DIM:, t:]], axis=0).T


def _fox_kernel(qa_ref, qb_ref, k_ref, v_ref, cr_ref, ct_ref, oa_ref, ob_ref,
                vt_ref, qs_ref, acc_ref, m_ref, l_ref,
                s0_ref, s1_ref, p0_ref, p1_ref, a0_ref, a1_ref, cq_ref, *, t, seq):
    pair = pl.program_id(1)
    qi = pl.program_id(2)
    nq = seq // t
    s_bufs, p_bufs, a_bufs = (s0_ref, s1_ref), (p0_ref, p1_ref), (a0_ref, a1_ref)

    @pl.when(qi == 0)
    def _():
        _build_vt(v_ref, vt_ref, seq)

    for tile, (q_ref, row) in enumerate(((qa_ref, qi), (qb_ref, nq - 1 - qi))):
        qs_ref[tile] = _stack_heads(q_ref[0], SCALE * LOG2E)
        qoff = pl.multiple_of(row * t, t)
        cq_ref[tile] = jnp.concatenate(
            [ct_ref[0, pl.ds(2 * pair + i, 1), pl.ds(qoff, t)] for i in range(2)], axis=1)
    _init_pair_state(acc_ref, m_ref, l_ref)
    where = _pair_steps(qi, nq)

    def logits(step, slot):
        tile, blk = where(step)
        s_bufs[slot][...] = _dot_nt(k_ref[0, pl.ds(pl.multiple_of(blk * t, t), t), :], qs_ref[tile])

    def softmax(step, slot, masked):
        tile, blk = where(step)
        off = pl.multiple_of(blk * t, t)
        ck = jnp.concatenate([jnp.tile(cr_ref[0, i, pl.ds(off, t), :], (1, t // LANES)) for i in range(2)],
                             axis=1)
        s = (s_bufs[slot][...] + cq_ref[tile]) - ck
        if masked:
            s = jnp.where(_stacked_causal(t, False), s, -jnp.inf)
        _softmax_stage(s, m_ref.at[tile], l_ref.at[tile], p_bufs[slot], a_bufs[slot])

    def pv(step, slot):
        tile, blk = where(step)
        vt = vt_ref[:, pl.ds(pl.multiple_of(blk * t, t), t)]
        acc_ref[tile] = a_bufs[slot][...] * acc_ref[tile] + _dot(vt, p_bufs[slot][...])

    _pipeline_pair(nq + 1, logits, softmax, pv)

    for tile, o_ref in enumerate((oa_ref, ob_ref)):
        o_ref[0] = _unstack_heads_t(acc_ref[tile] / l_ref[tile], t).astype(o_ref.dtype)


def _flash_t_scratch(s, t):
    row = pltpu.VMEM((1, 2 * t), jnp.float32)
    rows = pltpu.VMEM((2, 1, 2 * t), jnp.float32)
    return [pltpu.VMEM((LANES, s), jnp.bfloat16),
            pltpu.VMEM((2, 2 * t, LANES), jnp.bfloat16),
            pltpu.VMEM((2, LANES, 2 * t), jnp.float32),
            rows, rows,
            pltpu.VMEM((t, 2 * t), jnp.float32), pltpu.VMEM((t, 2 * t), jnp.float32),
            pltpu.VMEM((t, 2 * t), jnp.bfloat16), pltpu.VMEM((t, 2 * t), jnp.bfloat16),
            row, row]


def _fox_attention(p3, cum_rep, cum_t, *, t=512):
    b, s, _ = p3.shape
    t = min(t, s // 2)
    nq = s // t
    return _join_halves(*pl.pallas_call(
        functools.partial(_fox_kernel, t=t, seq=s),
        grid=(b, 2, nq // 2),
        in_specs=[
            pl.BlockSpec((1, t, LANES), lambda bi, p, qi: (bi, qi, BLK_BQ + p)),
            pl.BlockSpec((1, t, LANES), lambda bi, p, qi: (bi, nq - 1 - qi, BLK_BQ + p)),
            pl.BlockSpec((1, s, LANES), lambda bi, p, qi: (bi, 0, BLK_BK + p)),
            pl.BlockSpec((1, s, LANES), lambda bi, p, qi: (bi, 0, BLK_BV + p)),
            pl.BlockSpec((1, 2, s, LANES), lambda bi, p, qi: (bi, p, 0, 0)),
            pl.BlockSpec((1, 8, s), lambda bi, p, qi: (bi, 0, 0)),
        ],
        out_specs=[pl.BlockSpec((1, t, LANES), lambda bi, p, qi: (bi, qi, p)),
                   pl.BlockSpec((1, t, LANES), lambda bi, p, qi: (bi, nq // 2 - 1 - qi, p))],
        out_shape=_pair_out(b, s, 2 * LANES),
        scratch_shapes=_flash_t_scratch(s, t) + [pltpu.VMEM((2, 1, 2 * t), jnp.float32)],
        compiler_params=_cparams(("arbitrary", "arbitrary", "arbitrary")),
        name="fox_attn",
    )(p3, p3, p3, p3, cum_rep, cum_t))


def _sb_kernel(qa_ref, qb_ref, k_ref, v_ref, oa_ref, ob_ref,
               vt_ref, qs_ref, acc_ref, run_ref,
               s0_ref, s1_ref, p0_ref, p1_ref, *, t, seq):
    qi = pl.program_id(2)
    nq = seq // t
    s_bufs, p_bufs = (s0_ref, s1_ref), (p0_ref, p1_ref)

    @pl.when(qi == 0)
    def _():
        _build_vt(v_ref, vt_ref, seq)

    qs_ref[0] = _stack_heads(qa_ref[0], SCALE * LOG2E)
    qs_ref[1] = _stack_heads(qb_ref[0], SCALE * LOG2E)
    r = lax.broadcasted_iota(jnp.int32, (t, 2 * t), 0)
    c = lax.broadcasted_iota(jnp.int32, (t, 2 * t), 1)
    later2 = (jnp.where(c >= t, c - t, c) > r).astype(jnp.bfloat16)
    acc_ref[...] = jnp.zeros_like(acc_ref)
    run_ref[...] = jnp.zeros_like(run_ref)
    where = _pair_steps(qi, nq, reverse=True)

    def logits(step, slot):
        tile, blk = where(step)
        s_bufs[slot][...] = _dot_nt(k_ref[0, pl.ds(pl.multiple_of(blk * t, t), t), :], qs_ref[tile])

    def weights(step, slot, masked):
        tile, _ = where(step)
        z = s_bufs[slot][...]
        log_beta = jnp.minimum(z, 0.0) - jnp.log2(1.0 + jnp.exp2(-jnp.abs(z)))
        l1m = log_beta - z
        if masked:
            cm = _stacked_causal(t, True)
            l1m = jnp.where(cm, l1m, 0.0)
        l_hi = l1m.astype(jnp.bfloat16)
        l_lo = (l1m - l_hi.astype(jnp.float32)).astype(jnp.bfloat16)
        later_sum = _dot(later2, jnp.concatenate([l_hi, l_lo], axis=0))
        a = jnp.exp2(log_beta + (later_sum + run_ref[tile]))
        if masked:
            a = jnp.where(cm, a, 0.0)
        p_bufs[slot][...] = a.astype(jnp.bfloat16)
        run_ref[tile] += later_sum[0:1, :] + l1m[0:1, :]

    def pv(step, slot):
        tile, blk = where(step)
        acc_ref[tile] += _dot(vt_ref[:, pl.ds(pl.multiple_of(blk * t, t), t)], p_bufs[slot][...])

    _pipeline_pair(nq + 1, logits, weights, pv)

    for tile, o_ref in enumerate((oa_ref, ob_ref)):
        o_ref[0] = _unstack_heads_t(acc_ref[tile], t).astype(o_ref.dtype)


def _sb_attention(p3, *, t=256):
    b, s, _ = p3.shape
    t = min(t, s // 2)
    nq = s // t
    return _join_halves(*pl.pallas_call(
        functools.partial(_sb_kernel, t=t, seq=s),
        grid=(b, 2, nq // 2),
        in_specs=[
            pl.BlockSpec((1, t, LANES), lambda bi, p, qi: (bi, qi, BLK_CQ + p)),
            pl.BlockSpec((1, t, LANES), lambda bi, p, qi: (bi, nq - 1 - qi, BLK_CQ + p)),
            pl.BlockSpec((1, s, LANES), lambda bi, p, qi: (bi, 0, BLK_CK + p)),
            pl.BlockSpec((1, s, LANES), lambda bi, p, qi: (bi, 0, BLK_CV + p)),
        ],
        out_specs=[pl.BlockSpec((1, t, LANES), lambda bi, p, qi: (bi, qi, p)),
                   pl.BlockSpec((1, t, LANES), lambda bi, p, qi: (bi, nq // 2 - 1 - qi, p))],
        out_shape=_pair_out(b, s, 2 * LANES),
        scratch_shapes=[pltpu.VMEM((LANES, s), jnp.bfloat16),
                        pltpu.VMEM((2, 2 * t, LANES), jnp.bfloat16),
                        pltpu.VMEM((2, LANES, 2 * t), jnp.float32),
                        pltpu.VMEM((2, 1, 2 * t), jnp.float32),
                        pltpu.VMEM((t, 2 * t), jnp.float32), pltpu.VMEM((t, 2 * t), jnp.float32),
                        pltpu.VMEM((t, 2 * t), jnp.bfloat16), pltpu.VMEM((t, 2 * t), jnp.bfloat16)],
        compiler_params=_cparams(("arbitrary", "arbitrary", "arbitrary")),
        name="sb_attn",
    )(p3, p3, p3, p3))


def _dsa_kernel(iq_ref, q_ref, sm_ref, ik_ref, k_ref, v_ref, o_ref,
                keys_ref, bias_ref, vt_ref, sig_ref, acc_ref, m_ref, l_ref,
                s0_ref, s1_ref, *, tq, ck, topk, seq):
    qi = pl.program_id(1)
    nblk = (qi * tq + tq + ck - 1) // ck
    lo, hi = _half_masks(jnp.float32)
    halves = (lo, hi)

    @pl.when(qi == 0)
    def _():
        _build_vt(v_ref, vt_ref, seq)

    iw_t = sm_ref[0].T[SM_IW:SM_IW + N_IDX_HEADS, :]
    iq = iq_ref[0].astype(jnp.float32)
    iq_stack = jnp.concatenate(
        [(iq[:, (h // 2) * LANES:(h // 2 + 1) * LANES] * halves[h % 2]).astype(jnp.bfloat16)
         for h in range(N_IDX_HEADS)], axis=0)
    qds = [_stack_heads(q_ref[0][:, pr * LANES:(pr + 1) * LANES], SCALE * LOG2E) for pr in range(2)]
    t_idx = qi * tq + lax.broadcasted_iota(jnp.int32, (1, tq), 1)
    row = lax.broadcasted_iota(jnp.int32, (ck, tq), 0)

    def score_chunk(c, _):
        off = pl.multiple_of(c * ck, ck)
        ikc = ik_ref[0, pl.ds(off, ck), :]
        rel = _dot_nt(ikc, iq_stack)
        acc = jnp.zeros((ck, tq), jnp.float32)
        for h in range(N_IDX_HEADS):
            acc = acc + iw_t[h:h + 1, :] * jnp.maximum(rel[:, h * tq:(h + 1) * tq], 0.0)
        bits = pltpu.bitcast(acc, jnp.int32)
        key = bits ^ ((bits >> 31) & 0x7FFFFFFF)
        keys_ref[pl.ds(off, ck), :] = jnp.where(row + off <= t_idx, key, INT_MIN)
        return 0

    lax.fori_loop(0, nblk, score_chunk, 0)

    def count(pred_fn):
        def body(c, cnt):
            off = pl.multiple_of(c * ck, ck)
            hit = pred_fn(keys_ref[pl.ds(off, ck), :], off).astype(jnp.int32)
            lanes = 4
            accs = [hit[g * 8:(g + 1) * 8] for g in range(lanes)]
            for g in range(lanes, ck // 8):
                accs[g % lanes] = accs[g % lanes] + hit[g * 8:(g + 1) * 8]
            return cnt + ((accs[0] + accs[1]) + (accs[2] + accs[3]))
        cnt8 = lax.fori_loop(0, nblk, body, jnp.zeros((8, tq), jnp.int32))
        return jnp.sum(cnt8, axis=0, keepdims=True)

    c0 = count(lambda kk, off: kk >= 0)
    tau0 = jnp.where(c0 >= topk, 0, INT_MIN).astype(jnp.int32)
    cnt0 = jnp.where(c0 >= topk, c0, nblk * ck).astype(jnp.int32)

    def bit_step(i, carry):
        tau, cnt_ge = carry
        cand = tau + jnp.left_shift(jnp.int32(1), 30 - i)
        cnt = count(lambda kk, off: kk >= cand)
        ok = cnt >= topk
        return jnp.where(ok, cand, tau), jnp.where(ok, cnt, cnt_ge)

    tau, cnt_ge = lax.fori_loop(0, 31, bit_step, (tau0, cnt0))
    real = tau != INT_MIN
    sig_ref[...] = jnp.where(real, seq, -1).astype(jnp.int32)
    over = jnp.max(jnp.where(real & (cnt_ge > topk), 1, 0))

    @pl.when(over > 0)
    def _():
        cnt_gt = count(lambda kk, off: kk > tau)
        need = topk - cnt_gt

        def idx_step(i, x):
            cand = x + jnp.left_shift(jnp.int32(1), (seq.bit_length() - 2) - i)
            f = count(lambda kk, off: (kk == tau) & (row + off < cand))
            return jnp.where(f < need, cand, x)

        x = lax.fori_loop(0, seq.bit_length() - 1, idx_step, jnp.zeros((1, tq), jnp.int32))
        sig_ref[...] = jnp.where(real, x, -1)

    sigma = sig_ref[...]

    def bias_chunk(c, _):
        off = pl.multiple_of(c * ck, ck)
        kk = keys_ref[pl.ds(off, ck), :]
        sel = (kk > tau) | ((kk == tau) & (row + off <= sigma))
        bias_ref[pl.ds(off, ck), :] = jnp.where(sel, 0.0, -jnp.inf)
        return 0

    lax.fori_loop(0, nblk, bias_chunk, 0)

    s_bufs = (s0_ref, s1_ref)
    _init_pair_state(acc_ref, m_ref, l_ref)

    def logits(c, slot):
        off = pl.multiple_of(c * ck, ck)
        for pr in range(2):
            s_bufs[slot][:, 2 * pr * tq:2 * (pr + 1) * tq] = _dot_nt(
                k_ref[0, pl.ds(off, ck), pr * LANES:(pr + 1) * LANES], qds[pr])

    def consume(c, slot):
        off = pl.multiple_of(c * ck, ck)
        s = s_bufs[slot][...] + jnp.tile(bias_ref[pl.ds(off, ck), :], (1, 4))
        m = m_ref[...]
        m_new = jnp.maximum(m, jnp.max(s, axis=0, keepdims=True))
        alpha = jnp.exp2(m - m_new)
        p = jnp.exp2(s - m_new)
        l_ref[...] = alpha * l_ref[...] + jnp.sum(p, axis=0, keepdims=True)
        m_ref[...] = m_new
        p = p.astype(jnp.bfloat16)
        upd = jnp.concatenate(
            [_dot(vt_ref[h * HEAD_DIM:(h + 1) * HEAD_DIM, pl.ds(off, ck)], p[:, h * tq:(h + 1) * tq])
             for h in range(4)], axis=1)
        acc_ref[...] = alpha * acc_ref[...] + upd

    def pair_step(i, _):
        c = 2 * i
        logits(c + 1, 1)
        consume(c, 0)
        logits(jnp.minimum(c + 2, nblk - 1), 0)
        consume(c + 1, 1)
        return 0

    logits(0, 0)
    lax.fori_loop(0, nblk // 2, pair_step, 0)

    @pl.when(nblk % 2 == 1)
    def _():
        consume(nblk - 1, 0)

    out = acc_ref[...] / l_ref[...]
    out_t = jnp.concatenate([out[:, h * tq:(h + 1) * tq] for h in range(4)], axis=0)
    o_ref[0] = out_t.T.astype(o_ref.dtype)


def _dsa_attention(p3, small3, *, tq=256, ck=256):
    b, s, _ = p3.shape
    ck = min(ck, s)
    topk = min(TOPK_MAX, s // 4)
    return pl.pallas_call(
        functools.partial(_dsa_kernel, tq=tq, ck=ck, topk=topk, seq=s),
        grid=(b, s // tq),
        in_specs=[
            pl.BlockSpec((1, tq, 4 * LANES), lambda bi, qi: (bi, qi, BLK_IQ // 4)),
            pl.BlockSpec((1, tq, 2 * LANES), lambda bi, qi: (bi, qi, BLK_DQ // 2)),
            pl.BlockSpec((1, tq, LANES), lambda bi, qi: (bi, qi, 0)),
            pl.BlockSpec((1, s, LANES), lambda bi, qi: (bi, 0, BLK_IK)),
            pl.BlockSpec((1, s, 2 * LANES), lambda bi, qi: (bi, 0, BLK_DK // 2)),
            pl.BlockSpec((1, s, 2 * LANES), lambda bi, qi: (bi, 0, BLK_DV // 2)),
        ],
        out_specs=pl.BlockSpec((1, tq, 2 * LANES), lambda bi, qi: (bi, qi, 0)),
        out_shape=jax.ShapeDtypeStruct((b, s, 2 * LANES), jnp.bfloat16),
        scratch_shapes=[
            pltpu.VMEM((s, tq), jnp.int32),
            pltpu.VMEM((s, tq), jnp.float32),
            pltpu.VMEM((4 * HEAD_DIM, s), jnp.bfloat16),
            pltpu.VMEM((1, tq), jnp.int32),
            pltpu.VMEM((HEAD_DIM, 4 * tq), jnp.float32),
            pltpu.VMEM((1, 4 * tq), jnp.float32), pltpu.VMEM((1, 4 * tq), jnp.float32),
            pltpu.VMEM((ck, 4 * tq), jnp.float32), pltpu.VMEM((ck, 4 * tq), jnp.float32),
        ],
        compiler_params=_cparams(("arbitrary", "arbitrary")),
        name="dsa_attn",
    )(p3, p3, small3, p3, p3, p3)


def _merge_kernel(x_ref, g_ref, wgate_ref, bgate_ref, ya_ref, yb_ref, yc_ref, yd_ref,
                  wa_ref, wb_ref, wc_ref, wd_ref, wo_ref, o_ref):
    x = x_ref[...]
    d = x.shape[1]
    hb = _rms(x, g_ref[...], NORM_EPS).astype(jnp.bfloat16)
    merged = jnp.zeros(x.shape, jnp.float32)
    for i, (y_ref, w_ref) in enumerate(((ya_ref, wa_ref), (yb_ref, wb_ref), (yc_ref, wc_ref), (yd_ref, wd_ref))):
        gate = jax.nn.sigmoid(_dot(hb, wgate_ref[:, i * d:(i + 1) * d]) + bgate_ref[:, i * d:(i + 1) * d])
        merged = merged + gate * _dot(y_ref[...], w_ref[...])
    o_ref[...] = x + _dot(merged.astype(jnp.bfloat16), wo_ref[...])


def _merge(x2, gain, wgate, bgate, ya, yb, yc, yd, wa, wb, wc, wd, wo, *, tm=512):
    m, d = x2.shape
    rowblk = lambda w: pl.BlockSpec((tm, w), lambda i: (i, 0))
    return pl.pallas_call(
        _merge_kernel,
        grid=(m // tm,),
        in_specs=[
            rowblk(d), _resident((1, d)), _resident((d, N_BRANCH * d)), _resident((1, N_BRANCH * d)),
            rowblk(ya.shape[1]), rowblk(yb.shape[1]), rowblk(yc.shape[1]), rowblk(yd.shape[1]),
            _resident(wa.shape), _resident(wb.shape), _resident(wc.shape), _resident(wd.shape),
            _resident(wo.shape),
        ],
        out_specs=rowblk(d),
        out_shape=jax.ShapeDtypeStruct((m, d), jnp.float32),
        compiler_params=_cparams(("parallel",)),
        name="merge",
    )(x2, gain, wgate, bgate, ya, yb, yc, yd, wa, wb, wc, wd, wo)


def _proj_weights(w_in, b_fgt):
    d = w_in.shape[0]
    o = 0

    def take(width):
        nonlocal o
        seg = w_in[:, o:o + width]
        o += width
        return seg

    aq, ak, av = take(512), take(512), take(512)
    bq, bk, bv, bf = take(256), take(256), take(256), take(4)
    cq, ck, cv = take(256), take(256), take(256)
    dq, dk, dv = take(256), take(256), take(256)
    diq, dik, diw = take(512), take(64), take(8)
    zeros = jnp.zeros((d, LANES), w_in.dtype)
    big = jnp.concatenate([aq, ak, av, bq, bk, bv, cq, ck, cv, dq, dk, dv, dik, dik, zeros, diq], axis=1)
    small = jnp.concatenate([diw, bf, jnp.zeros((d, LANES - 12), w_in.dtype)], axis=1)
    bias = jnp.concatenate([jnp.zeros((8,), jnp.float32), b_fgt.astype(jnp.float32),
                            jnp.zeros((LANES - 12,), jnp.float32)]).reshape(1, LANES)
    return big.astype(jnp.bfloat16), small.astype(jnp.bfloat16), bias


def kernel(x, positions, ffn1_norm, ffn1_w_gu, ffn1_w_down, mix_norm, w_in, b_fgt, lam_q1, lam_k1, lam_q2, lam_k2, diff_gain, w_gate, b_gate, w_br_a, w_br_b, w_br_c, w_br_d, w_out, ffn2_norm, ffn2_w_gu, ffn2_w_down, final_norm):
    b, s, d = x.shape
    depth = w_in.shape[0]
    bf = jnp.bfloat16
    cos_t, sin1_t, sin2_t = _rope_tables(positions)
    fgain = final_norm.reshape(1, d)
    x2 = x.reshape(b * s, d)
    for l in range(depth):
        x2 = _ffn(x2, ffn1_norm[l].reshape(1, d), ffn1_w_gu[l][:, :D_FF].astype(bf),
                  ffn1_w_gu[l][:, D_FF:].astype(bf), ffn1_w_down[l].astype(bf), fgain, final_norm=False)
        wbig, wsmall, bsmall = _proj_weights(w_in[l], b_fgt[l])
        p2, small2 = _proj(x2, mix_norm[l].reshape(1, d), wbig, wsmall, bsmall, cos_t, sin1_t, sin2_t, s)
        p3 = p2.reshape(b, s, PROJ_W)
        small3 = small2.reshape(b, s, LANES)
        cum_n, cum_t = _cumsum(small3)
        lam_rows = jnp.zeros((8, LANES), jnp.float32)
        lam_rows = lam_rows.at[0:4, 0:HEAD_DIM].set(jnp.stack([lam_q1[l], lam_k1[l], lam_q2[l], lam_k2[l]]))
        ya = _diff_attention(p3, lam_rows, diff_gain[l].reshape(1, LANES), l)
        yb = _fox_attention(p3, cum_n, cum_t)
        yc = _sb_attention(p3)
        yd = _dsa_attention(p3, small3)
        x2 = _merge(x2, mix_norm[l].reshape(1, d), w_gate[l].astype(bf), b_gate[l].reshape(1, N_BRANCH * d),
                    ya.reshape(b * s, -1), yb.reshape(b * s, -1), yc.reshape(b * s, -1), yd.reshape(b * s, -1),
                    w_br_a[l].astype(bf), w_br_b[l].astype(bf), w_br_c[l].astype(bf), w_br_d[l].astype(bf),
                    w_out[l].astype(bf))
        x2 = _ffn(x2, ffn2_norm[l].reshape(1, d), ffn2_w_gu[l][:, :D_FF].astype(bf),
                  ffn2_w_gu[l][:, D_FF:].astype(bf), ffn2_w_down[l].astype(bf), fgain,
                  final_norm=(l == depth - 1))
    return x2.reshape(b, s, d)
```

```python
import functools
import math

import numpy as np
import jax
import jax.numpy as jnp
from jax import lax
from jax.experimental import pallas as pl
from jax.experimental.pallas import tpu as pltpu

D_MODEL = 1024
HEAD_DIM = 64
N_IDX_HEADS = 8
TOPK_MAX = 256
ROPE_THETA = 500000.0
ROPE_DIM = HEAD_DIM // 4
D_FF = 2816
N_BRANCH = 4
NORM_EPS = 1e-6
SUBLN_EPS = 1e-5
SCALE = HEAD_DIM ** -0.5
LOG2E = math.log2(math.e)

LANES = 128
VMEM_LIMIT = 56 * 1024 * 1024

BLK_AQ, BLK_AK, BLK_AV = 0, 4, 8
BLK_BQ, BLK_BK, BLK_BV = 12, 14, 16
BLK_CQ, BLK_CK, BLK_CV = 18, 20, 22
BLK_DQ, BLK_DK, BLK_DV = 24, 26, 28
BLK_IK, BLK_SPARE, BLK_IQ = 30, 31, 32
N_PROJ_BLKS = 36
PROJ_W = N_PROJ_BLKS * LANES
ROPE_BLKS = frozenset(list(range(0, 8)) + [24, 25, 26, 27, 30] + list(range(32, 36)))
SM_IW, SM_F = 0, 8

NEG_BIG = -1e30
INT_MIN = -(2 ** 31)


def _cparams(sem, vmem=VMEM_LIMIT):
    return pltpu.CompilerParams(dimension_semantics=sem, vmem_limit_bytes=vmem)


def _resident(shape):
    nd = len(shape)
    return pl.BlockSpec(shape, lambda *_: (0,) * nd, pipeline_mode=pl.Buffered(1))


def _rms(x, g, eps):
    ms = jnp.mean(x * x, axis=-1, keepdims=True)
    return x * lax.rsqrt(ms + eps) * g


def _dot(a, b):
    return jnp.dot(a, b, preferred_element_type=jnp.float32)


def _dot_nt(a, b):
    return lax.dot_general(a, b, (((1,), (1,)), ((), ())), preferred_element_type=jnp.float32)


def _ffn_kernel(x_ref, g_ref, wg_ref, wu_ref, wd_ref, fg_ref, o_ref, *, tf, final_norm):
    x = x_ref[...]
    hb = _rms(x, g_ref[...], NORM_EPS).astype(jnp.bfloat16)
    acc = jnp.zeros(x.shape, jnp.float32)
    for j in range(D_FF // tf):
        sl = slice(j * tf, (j + 1) * tf)
        g = _dot(hb, wg_ref[:, sl])
        u = _dot(hb, wu_ref[:, sl])
        a = (g * jax.nn.sigmoid(g) * u).astype(jnp.bfloat16)
        acc = acc + _dot(a, wd_ref[sl, :])
    y = x + 0.5 * acc
    if final_norm:
        y = _rms(y, fg_ref[...], NORM_EPS)
    o_ref[...] = y


def _ffn(x2, gain, wg, wu, wd, fgain, *, final_norm, tm=512, tf=256):
    m, d = x2.shape
    return pl.pallas_call(
        functools.partial(_ffn_kernel, tf=tf, final_norm=final_norm),
        grid=(m // tm,),
        in_specs=[
            pl.BlockSpec((tm, d), lambda i: (i, 0)),
            _resident((1, d)),
            _resident((d, D_FF)),
            _resident((d, D_FF)),
            _resident((D_FF, d)),
            _resident((1, d)),
        ],
        out_specs=pl.BlockSpec((tm, d), lambda i: (i, 0)),
        out_shape=jax.ShapeDtypeStruct((m, d), jnp.float32),
        compiler_params=_cparams(("parallel",)),
        name="ffn",
    )(x2, gain, wg, wu, wd, fgain)


def _rope_kernel(pos_ref, freq_ref, m1_ref, m2_ref, c_ref, s1_ref, s2_ref):
    ang = pos_ref[...].astype(jnp.float32) * freq_ref[...]
    c = jnp.cos(ang)
    s = jnp.sin(ang)
    c_ref[...] = c
    s1_ref[...] = s * m1_ref[...]
    s2_ref[...] = s * m2_ref[...]


def _rope_tables(positions):
    s = positions.shape[0]
    freqs = ROPE_THETA ** (-jnp.arange(0, ROPE_DIM, 2, dtype=jnp.float32) / ROPE_DIM)
    half = ROPE_DIM // 2
    d = np.arange(LANES) % HEAD_DIM
    freq_row = jnp.where(d < ROPE_DIM, freqs[d % half], 0.0).reshape(1, LANES)
    m1 = jnp.asarray(np.where(d < half, -1.0, 0.0).astype(np.float32)).reshape(1, LANES)
    m2 = jnp.asarray(np.where((d >= half) & (d < ROPE_DIM), 1.0, 0.0).astype(np.float32)).reshape(1, LANES)
    tab = jax.ShapeDtypeStruct((s, LANES), jnp.float32)
    ts = min(s, 1024)
    row = pl.BlockSpec((1, LANES), lambda i: (0, 0))
    blk = pl.BlockSpec((ts, LANES), lambda i: (i, 0))
    return pl.pallas_call(
        _rope_kernel,
        grid=(s // ts,),
        in_specs=[pl.BlockSpec((ts, 1), lambda i: (i, 0)), row, row, row],
        out_specs=[blk, blk, blk],
        out_shape=[tab, tab, tab],
        compiler_params=_cparams(("parallel",)),
        name="rope_tables",
    )(positions.reshape(s, 1), freq_row, m1, m2)


def _proj_kernel(x_ref, g_ref, w_ref, wsm_ref, bsm_ref, c_ref, s1_ref, s2_ref, p_ref, sm_ref):
    hb = _rms(x_ref[...], g_ref[...], NORM_EPS).astype(jnp.bfloat16)
    c, s1, s2 = c_ref[...], s1_ref[...], s2_ref[...]
    chunk = 4
    for cb in range(N_PROJ_BLKS // chunk):
        y = _dot(hb, w_ref[:, cb * chunk * LANES:(cb + 1) * chunk * LANES])
        for sub in range(chunk):
            blk = cb * chunk + sub
            ys = y[:, sub * LANES:(sub + 1) * LANES]
            if blk in ROPE_BLKS:
                ys = (ys * c + pltpu.roll(ys, LANES - ROPE_DIM // 2, 1) * s1
                      + pltpu.roll(ys, ROPE_DIM // 2, 1) * s2)
            p_ref[:, blk * LANES:(blk + 1) * LANES] = ys.astype(jnp.bfloat16)
    sm_ref[...] = _dot(hb, wsm_ref[...]) + bsm_ref[...]


def _proj(x2, gain, w, wsm, bsm, c, s1, s2, seq, *, tm=512):
    m, d = x2.shape
    tm = min(tm, seq)
    nrep = seq // tm
    tab = pl.BlockSpec((tm, LANES), lambda i: (i % nrep, 0))
    return pl.pallas_call(
        _proj_kernel,
        grid=(m // tm,),
        in_specs=[
            pl.BlockSpec((tm, d), lambda i: (i, 0)),
            _resident((1, d)),
            _resident((d, PROJ_W)),
            _resident((d, LANES)),
            _resident((1, LANES)),
            tab, tab, tab,
        ],
        out_specs=[pl.BlockSpec((tm, PROJ_W), lambda i: (i, 0)),
                   pl.BlockSpec((tm, LANES), lambda i: (i, 0))],
        out_shape=[jax.ShapeDtypeStruct((m, PROJ_W), jnp.bfloat16),
                   jax.ShapeDtypeStruct((m, LANES), jnp.float32)],
        compiler_params=_cparams(("parallel",)),
        name="proj",
    )(x2, gain, w, wsm, bsm, c, s1, s2)


def _cumsum_kernel(sm_ref, cn_ref, ct_ref):
    t = sm_ref[0].T
    n = t.shape[1]
    ls = jnp.minimum(t, 0.0) - jnp.log(1.0 + jnp.exp(-jnp.abs(t)))
    lane = lax.broadcasted_iota(jnp.int32, ls.shape, 1)
    sh = 1
    while sh < n:
        ls = ls + jnp.where(lane >= sh, pltpu.roll(ls, sh, 1), 0.0)
        sh *= 2
    ls = ls * LOG2E
    ct_ref[0] = ls[SM_F:SM_F + 8, :]
    for h in range(4):
        cn_ref[0, h] = jnp.broadcast_to(ls[SM_F + h:SM_F + h + 1, :], ls.shape).T


def _cumsum(small3):
    b, s, _ = small3.shape
    return pl.pallas_call(
        _cumsum_kernel,
        grid=(b,),
        in_specs=[pl.BlockSpec((1, s, LANES), lambda i: (i, 0, 0))],
        out_specs=[pl.BlockSpec((1, 4, s, LANES), lambda i: (i, 0, 0, 0)),
                   pl.BlockSpec((1, 8, s), lambda i: (i, 0, 0))],
        out_shape=[jax.ShapeDtypeStruct((b, 4, s, LANES), jnp.float32),
                   jax.ShapeDtypeStruct((b, 8, s), jnp.float32)],
        compiler_params=_cparams(("parallel",)),
        name="forget_cumsum",
    )(small3)


def _half_masks(dtype):
    lane = lax.broadcasted_iota(jnp.int32, (1, LANES), 1)
    lo = (lane < HEAD_DIM).astype(dtype)
    return lo, (1 - lo).astype(dtype)


def _pipeline_pair(n_steps, logits, weights, pv):
    assert n_steps % 2 == 1 and n_steps >= 3
    logits(0, 0)
    logits(1, 1)
    weights(0, 0, True)
    logits(2, 0)
    weights(1, 1, True)
    pv(0, 0)

    def body(i, _):
        s = 2 * i
        logits(s + 1, 1)
        weights(s, 0, False)
        pv(s - 1, 1)
        logits(s + 2, 0)
        weights(s + 1, 1, False)
        pv(s, 0)
        return 0

    lax.fori_loop(1, n_steps // 2, body, 0)
    pv(n_steps - 2, 1)
    weights(n_steps - 1, 0, False)
    pv(n_steps - 1, 0)


def _pair_steps(qi, nq, reverse=False):
    def where(step):
        u = step - 2
        in_b = jnp.where(step < 2, step == 1, u >= qi)
        v = jnp.where(in_b, u - qi, u)
        n_other = jnp.where(in_b, nq - 1 - qi, qi)
        blk = (n_other - 1 - v) if reverse else v
        return in_b.astype(jnp.int32), jnp.where(step < 2, n_other, blk)

    return where


def _build_vt(v_ref, vt_ref, seq, chunk=512):
    chunk = min(chunk, seq)
    for c in range(seq // chunk):
        blk = v_ref[0, c * chunk:(c + 1) * chunk, :].astype(jnp.float32)
        vt_ref[:, c * chunk:(c + 1) * chunk] = blk.T.astype(vt_ref.dtype)


def _softmax_stage(s, m_ref, l_ref, p_buf, a_buf):
    m = m_ref[...]
    m_new = jnp.maximum(m, jnp.max(s, axis=0, keepdims=True))
    alpha = jnp.exp2(m - m_new)
    p = jnp.exp2(s - m_new)
    l_ref[...] = alpha * l_ref[...] + jnp.sum(p, axis=0, keepdims=True)
    m_ref[...] = m_new
    a_buf[...] = alpha
    p_buf[...] = p.astype(jnp.bfloat16)


def _init_pair_state(acc_ref, m_ref, l_ref):
    acc_ref[...] = jnp.zeros_like(acc_ref)
    m_ref[...] = jnp.full(m_ref.shape, NEG_BIG, jnp.float32)
    l_ref[...] = jnp.zeros_like(l_ref)


def _diff_kernel(qa_ref, qb_ref, k_ref, v_ref, lam_ref, gain_ref, oa_ref, ob_ref,
                 vt_ref, qs_ref, acc_ref, m_ref, l_ref,
                 s0_ref, s1_ref, p0_ref, p1_ref, a0_ref, a1_ref, *, t, lam_init, seq):
    qi = pl.program_id(2)
    nq = seq // t
    s_bufs, p_bufs, a_bufs = (s0_ref, s1_ref), (p0_ref, p1_ref), (a0_ref, a1_ref)

    @pl.when(qi == 0)
    def _():
        _build_vt(v_ref, vt_ref, seq)

    qs_ref[0] = _stack_heads(qa_ref[0], SCALE * LOG2E)
    qs_ref[1] = _stack_heads(qb_ref[0], SCALE * LOG2E)
    _init_pair_state(acc_ref, m_ref, l_ref)
    where = _pair_steps(qi, nq)

    def logits(step, slot):
        tile, blk = where(step)
        s_bufs[slot][...] = _dot_nt(k_ref[0, pl.ds(pl.multiple_of(blk * t, t), t), :], qs_ref[tile])

    def softmax(step, slot, masked):
        tile, _ = where(step)
        s = s_bufs[slot][...]
        if masked:
            s = jnp.where(_stacked_causal(t, False), s, -jnp.inf)
        _softmax_stage(s, m_ref.at[tile], l_ref.at[tile], p_bufs[slot], a_bufs[slot])

    def pv(step, slot):
        tile, blk = where(step)
        vt = vt_ref[:, pl.ds(pl.multiple_of(blk * t, t), t)]
        acc_ref[tile] = a_bufs[slot][...] * acc_ref[tile] + _dot(vt, p_bufs[slot][...])

    _pipeline_pair(nq + 1, logits, softmax, pv)

    lp = lam_ref[...]
    lam = (jnp.exp(jnp.sum(lp[0:1] * lp[1:2], axis=-1, keepdims=True))
           - jnp.exp(jnp.sum(lp[2:3] * lp[3:4], axis=-1, keepdims=True)) + lam_init)
    for tile, o_ref in enumerate((oa_ref, ob_ref)):
        yt = acc_ref[tile] / l_ref[tile]
        y = (yt[:, :t] - lam * yt[:, t:]).T
        y = _rms(y, gain_ref[...], SUBLN_EPS) * (1.0 - lam_init)
        o_ref[0] = y.astype(o_ref.dtype)


def _pair_out(b, s, width):
    return [jax.ShapeDtypeStruct((b, s // 2, width), jnp.bfloat16)] * 2


def _join_halves(lo, hi):
    return jnp.concatenate([lo, hi], axis=1)


def _diff_attention(p3, lam_rows, gain, layer_idx, *, t=512):
    b, s, _ = p3.shape
    t = min(t, s // 2)
    nq = s // t
    lam_init = 0.8 - 0.6 * math.exp(-0.3 * layer_idx)
    nh = 4
    return _join_halves(*pl.pallas_call(
        functools.partial(_diff_kernel, t=t, lam_init=lam_init, seq=s),
        grid=(b, nh, nq // 2),
        in_specs=[
            pl.BlockSpec((1, t, LANES), lambda bi, h, qi: (bi, qi, BLK_AQ + h)),
            pl.BlockSpec((1, t, LANES), lambda bi, h, qi: (bi, nq - 1 - qi, BLK_AQ + h)),
            pl.BlockSpec((1, s, LANES), lambda bi, h, qi: (bi, 0, BLK_AK + h)),
            pl.BlockSpec((1, s, LANES), lambda bi, h, qi: (bi, 0, BLK_AV + h)),
            pl.BlockSpec((8, LANES), lambda bi, h, qi: (0, 0)),
            pl.BlockSpec((1, LANES), lambda bi, h, qi: (0, 0)),
        ],
        out_specs=[pl.BlockSpec((1, t, LANES), lambda bi, h, qi: (bi, qi, h)),
                   pl.BlockSpec((1, t, LANES), lambda bi, h, qi: (bi, nq // 2 - 1 - qi, h))],
        out_shape=_pair_out(b, s, nh * LANES),
        scratch_shapes=_flash_t_scratch(s, t),
        compiler_params=_cparams(("arbitrary", "arbitrary", "arbitrary")),
        name="diff_attn",
    )(p3, p3, p3, p3, lam_rows, gain))


def _stack_heads(q, scale):
    lo, hi = _half_masks(jnp.float32)
    q = q.astype(jnp.float32) * scale
    return jnp.concatenate([(q * lo).astype(jnp.bfloat16), (q * hi).astype(jnp.bfloat16)], axis=0)


def _stacked_causal(t, strict):
    kr = lax.broadcasted_iota(jnp.int32, (t, 2 * t), 0)
    qc = lax.broadcasted_iota(jnp.int32, (t, 2 * t), 1)
    qc = jnp.where(qc >= t, qc - t, qc)
    return (kr < qc) if strict else (kr <= qc)


def _unstack_heads_t(acc, t):
    return jnp.concatenate([acc[:HEAD_DIM, :t], acc[HEAD_DIM:, t:]], axis=0).T


def _fox_kernel(qa_ref, qb_ref, k_ref, v_ref, cr_ref, ct_ref, oa_ref, ob_ref,
                vt_ref, qs_ref, acc_ref, m_ref, l_ref,
                s0_ref, s1_ref, p0_ref, p1_ref, a0_ref, a1_ref, cq_ref, *, t, seq):
    pair = pl.program_id(1)
    qi = pl.program_id(2)
    nq = seq // t
    s_bufs, p_bufs, a_bufs = (s0_ref, s1_ref), (p0_ref, p1_ref), (a0_ref, a1_ref)

    @pl.when(qi == 0)
    def _():
        _build_vt(v_ref, vt_ref, seq)

    for tile, (q_ref, row) in enumerate(((qa_ref, qi), (qb_ref, nq - 1 - qi))):
        qs_ref[tile] = _stack_heads(q_ref[0], SCALE * LOG2E)
        qoff = pl.multiple_of(row * t, t)
        cq_ref[tile] = jnp.concatenate(
            [ct_ref[0, pl.ds(2 * pair + i, 1), pl.ds(qoff, t)] for i in range(2)], axis=1)
    _init_pair_state(acc_ref, m_ref, l_ref)
    where = _pair_steps(qi, nq)

    def logits(step, slot):
        tile, blk = where(step)
        s_bufs[slot][...] = _dot_nt(k_ref[0, pl.ds(pl.multiple_of(blk * t, t), t), :], qs_ref[tile])

    def softmax(step, slot, masked):
        tile, blk = where(step)
        off = pl.multiple_of(blk * t, t)
        ck = jnp.concatenate([jnp.tile(cr_ref[0, i, pl.ds(off, t), :], (1, t // LANES)) for i in range(2)],
                             axis=1)
        s = (s_bufs[slot][...] + cq_ref[tile]) - ck
        if masked:
            s = jnp.where(_stacked_causal(t, False), s, -jnp.inf)
        _softmax_stage(s, m_ref.at[tile], l_ref.at[tile], p_bufs[slot], a_bufs[slot])

    def pv(step, slot):
        tile, blk = where(step)
        vt = vt_ref[:, pl.ds(pl.multiple_of(blk * t, t), t)]
        acc_ref[tile] = a_bufs[slot][...] * acc_ref[tile] + _dot(vt, p_bufs[slot][...])

    _pipeline_pair(nq + 1, logits, softmax, pv)

    for tile, o_ref in enumerate((oa_ref, ob_ref)):
        o_ref[0] = _unstack_heads_t(acc_ref[tile] / l_ref[tile], t).astype(o_ref.dtype)


def _flash_t_scratch(s, t):
    row = pltpu.VMEM((1, 2 * t), jnp.float32)
    rows = pltpu.VMEM((2, 1, 2 * t), jnp.float32)
    return [pltpu.VMEM((LANES, s), jnp.bfloat16),
            pltpu.VMEM((2, 2 * t, LANES), jnp.bfloat16),
            pltpu.VMEM((2, LANES, 2 * t), jnp.float32),
            rows, rows,
            pltpu.VMEM((t, 2 * t), jnp.float32), pltpu.VMEM((t, 2 * t), jnp.float32),
            pltpu.VMEM((t, 2 * t), jnp.bfloat16), pltpu.VMEM((t, 2 * t), jnp.bfloat16),
            row, row]


def _fox_attention(p3, cum_rep, cum_t, *, t=512):
    b, s, _ = p3.shape
    t = min(t, s // 2)
    nq = s // t
    return _join_halves(*pl.pallas_call(
        functools.partial(_fox_kernel, t=t, seq=s),
        grid=(b, 2, nq // 2),
        in_specs=[
            pl.BlockSpec((1, t, LANES), lambda bi, p, qi: (bi, qi, BLK_BQ + p)),
            pl.BlockSpec((1, t, LANES), lambda bi, p, qi: (bi, nq - 1 - qi, BLK_BQ + p)),
            pl.BlockSpec((1, s, LANES), lambda bi, p, qi: (bi, 0, BLK_BK + p)),
            pl.BlockSpec((1, s, LANES), lambda bi, p, qi: (bi, 0, BLK_BV + p)),
            pl.BlockSpec((1, 2, s, LANES), lambda bi, p, qi: (bi, p, 0, 0)),
            pl.BlockSpec((1, 8, s), lambda bi, p, qi: (bi, 0, 0)),
        ],
        out_specs=[pl.BlockSpec((1, t, LANES), lambda bi, p, qi: (bi, qi, p)),
                   pl.BlockSpec((1, t, LANES), lambda bi, p, qi: (bi, nq // 2 - 1 - qi, p))],
        out_shape=_pair_out(b, s, 2 * LANES),
        scratch_shapes=_flash_t_scratch(s, t) + [pltpu.VMEM((2, 1, 2 * t), jnp.float32)],
        compiler_params=_cparams(("arbitrary", "arbitrary", "arbitrary")),
        name="fox_attn",
    )(p3, p3, p3, p3, cum_rep, cum_t))


def _sb_kernel(qa_ref, qb_ref, k_ref, v_ref, oa_ref, ob_ref,
               vt_ref, qs_ref, acc_ref, run_ref,
               s0_ref, s1_ref, p0_ref, p1_ref, *, t, seq):
    qi = pl.program_id(2)
    nq = seq // t
    s_bufs, p_bufs = (s0_ref, s1_ref), (p0_ref, p1_ref)

    @pl.when(qi == 0)
    def _():
        _build_vt(v_ref, vt_ref, seq)

    qs_ref[0] = _stack_heads(qa_ref[0], SCALE * LOG2E)
    qs_ref[1] = _stack_heads(qb_ref[0], SCALE * LOG2E)
    r = lax.broadcasted_iota(jnp.int32, (t, 2 * t), 0)
    c = lax.broadcasted_iota(jnp.int32, (t, 2 * t), 1)
    later2 = (jnp.where(c >= t, c - t, c) > r).astype(jnp.bfloat16)
    acc_ref[...] = jnp.zeros_like(acc_ref)
    run_ref[...] = jnp.zeros_like(run_ref)
    where = _pair_steps(qi, nq, reverse=True)

    def logits(step, slot):
        tile, blk = where(step)
        s_bufs[slot][...] = _dot_nt(k_ref[0, pl.ds(pl.multiple_of(blk * t, t), t), :], qs_ref[tile])

    def weights(step, slot, masked):
        tile, _ = where(step)
        z = s_bufs[slot][...]
        log_beta = jnp.minimum(z, 0.0) - jnp.log2(1.0 + jnp.exp2(-jnp.abs(z)))
        l1m = log_beta - z
        if masked:
            cm = _stacked_causal(t, True)
            l1m = jnp.where(cm, l1m, 0.0)
        l_hi = l1m.astype(jnp.bfloat16)
        l_lo = (l1m - l_hi.astype(jnp.float32)).astype(jnp.bfloat16)
        later_sum = _dot(later2, jnp.concatenate([l_hi, l_lo], axis=0))
        a = jnp.exp2(log_beta + (later_sum + run_ref[tile]))
        if masked:
            a = jnp.where(cm, a, 0.0)
        p_bufs[slot][...] = a.astype(jnp.bfloat16)
        run_ref[tile] += later_sum[0:1, :] + l1m[0:1, :]

    def pv(step, slot):
        tile, blk = where(step)
        acc_ref[tile] += _dot(vt_ref[:, pl.ds(pl.multiple_of(blk * t, t), t)], p_bufs[slot][...])

    _pipeline_pair(nq + 1, logits, weights, pv)

    for tile, o_ref in enumerate((oa_ref, ob_ref)):
        o_ref[0] = _unstack_heads_t(acc_ref[tile], t).astype(o_ref.dtype)


def _sb_attention(p3, *, t=512):
    b, s, _ = p3.shape
    t = min(t, s // 2)
    nq = s // t
    return _join_halves(*pl.pallas_call(
        functools.partial(_sb_kernel, t=t, seq=s),
        grid=(b, 2, nq // 2),
        in_specs=[
            pl.BlockSpec((1, t, LANES), lambda bi, p, qi: (bi, qi, BLK_CQ + p)),
            pl.BlockSpec((1, t, LANES), lambda bi, p, qi: (bi, nq - 1 - qi, BLK_CQ + p)),
            pl.BlockSpec((1, s, LANES), lambda bi, p, qi: (bi, 0, BLK_CK + p)),
            pl.BlockSpec((1, s, LANES), lambda bi, p, qi: (bi, 0, BLK_CV + p)),
        ],
        out_specs=[pl.BlockSpec((1, t, LANES), lambda bi, p, qi: (bi, qi, p)),
                   pl.BlockSpec((1, t, LANES), lambda bi, p, qi: (bi, nq // 2 - 1 - qi, p))],
        out_shape=_pair_out(b, s, 2 * LANES),
        scratch_shapes=[pltpu.VMEM((LANES, s), jnp.bfloat16),
                        pltpu.VMEM((2, 2 * t, LANES), jnp.bfloat16),
                        pltpu.VMEM((2, LANES, 2 * t), jnp.float32),
                        pltpu.VMEM((2, 1, 2 * t), jnp.float32),
                        pltpu.VMEM((t, 2 * t), jnp.float32), pltpu.VMEM((t, 2 * t), jnp.float32),
                        pltpu.VMEM((t, 2 * t), jnp.bfloat16), pltpu.VMEM((t, 2 * t), jnp.bfloat16)],
        compiler_params=_cparams(("arbitrary", "arbitrary", "arbitrary")),
        name="sb_attn",
    )(p3, p3, p3, p3))


def _dsa_kernel(iq_ref, q_ref, sm_ref, ik_ref, k_ref, v_ref, o_ref,
                keys_ref, bias_ref, vt_ref, sig_ref, acc_ref, m_ref, l_ref,
                s0_ref, s1_ref, *, tq, ck, topk, seq):
    qi = pl.program_id(1)
    nblk = (qi * tq + tq + ck - 1) // ck
    lo, hi = _half_masks(jnp.float32)
    halves = (lo, hi)

    @pl.when(qi == 0)
    def _():
        _build_vt(v_ref, vt_ref, seq)

    iw_t = sm_ref[0].T[SM_IW:SM_IW + N_IDX_HEADS, :]
    iq = iq_ref[0].astype(jnp.float32)
    iq_stack = jnp.concatenate(
        [(iq[:, (h // 2) * LANES:(h // 2 + 1) * LANES] * halves[h % 2]).astype(jnp.bfloat16)
         for h in range(N_IDX_HEADS)], axis=0)
    qds = [_stack_heads(q_ref[0][:, pr * LANES:(pr + 1) * LANES], SCALE * LOG2E) for pr in range(2)]
    t_idx = qi * tq + lax.broadcasted_iota(jnp.int32, (1, tq), 1)
    row = lax.broadcasted_iota(jnp.int32, (ck, tq), 0)

    def score_chunk(c, _):
        off = pl.multiple_of(c * ck, ck)
        ikc = ik_ref[0, pl.ds(off, ck), :]
        rel = _dot_nt(ikc, iq_stack)
        acc = jnp.zeros((ck, tq), jnp.float32)
        for h in range(N_IDX_HEADS):
            acc = acc + iw_t[h:h + 1, :] * jnp.maximum(rel[:, h * tq:(h + 1) * tq], 0.0)
        bits = pltpu.bitcast(acc, jnp.int32)
        key = bits ^ ((bits >> 31) & 0x7FFFFFFF)
        keys_ref[pl.ds(off, ck), :] = jnp.where(row + off <= t_idx, key, INT_MIN)
        return 0

    lax.fori_loop(0, nblk, score_chunk, 0)

    def count(pred_fn):
        def body(c, cnt):
            off = pl.multiple_of(c * ck, ck)
            hit = pred_fn(keys_ref[pl.ds(off, ck), :], off).astype(jnp.int32)
            lanes = 4
            accs = [hit[g * 8:(g + 1) * 8] for g in range(lanes)]
            for g in range(lanes, ck // 8):
                accs[g % lanes] = accs[g % lanes] + hit[g * 8:(g + 1) * 8]
            return cnt + ((accs[0] + accs[1]) + (accs[2] + accs[3]))
        cnt8 = lax.fori_loop(0, nblk, body, jnp.zeros((8, tq), jnp.int32))
        return jnp.sum(cnt8, axis=0, keepdims=True)

    c0 = count(lambda kk, off: kk >= 0)
    tau0 = jnp.where(c0 >= topk, 0, INT_MIN).astype(jnp.int32)
    cnt0 = jnp.where(c0 >= topk, c0, nblk * ck).astype(jnp.int32)

    def bit_step(i, carry):
        tau, cnt_ge = carry
        cand = tau + jnp.left_shift(jnp.int32(1), 30 - i)
        cnt = count(lambda kk, off: kk >= cand)
        ok = cnt >= topk
        return jnp.where(ok, cand, tau), jnp.where(ok, cnt, cnt_ge)

    tau, cnt_ge = lax.fori_loop(0, 31, bit_step, (tau0, cnt0))
    real = tau != INT_MIN
    sig_ref[...] = jnp.where(real, seq, -1).astype(jnp.int32)
    over = jnp.max(jnp.where(real & (cnt_ge > topk), 1, 0))

    @pl.when(over > 0)
    def _():
        cnt_gt = count(lambda kk, off: kk > tau)
        need = topk - cnt_gt

        def idx_step(i, x):
            cand = x + jnp.left_shift(jnp.int32(1), (seq.bit_length() - 2) - i)
            f = count(lambda kk, off: (kk == tau) & (row + off < cand))
            return jnp.where(f < need, cand, x)

        x = lax.fori_loop(0, seq.bit_length() - 1, idx_step, jnp.zeros((1, tq), jnp.int32))
        sig_ref[...] = jnp.where(real, x, -1)

    sigma = sig_ref[...]

    def bias_chunk(c, _):
        off = pl.multiple_of(c * ck, ck)
        kk = keys_ref[pl.ds(off, ck), :]
        sel = (kk > tau) | ((kk == tau) & (row + off <= sigma))
        bias_ref[pl.ds(off, ck), :] = jnp.where(sel, 0.0, -jnp.inf)
        return 0

    lax.fori_loop(0, nblk, bias_chunk, 0)

    s_bufs = (s0_ref, s1_ref)
    _init_pair_state(acc_ref, m_ref, l_ref)

    def logits(c, slot):
        off = pl.multiple_of(c * ck, ck)
        for pr in range(2):
            s_bufs[slot][:, 2 * pr * tq:2 * (pr + 1) * tq] = _dot_nt(
                k_ref[0, pl.ds(off, ck), pr * LANES:(pr + 1) * LANES], qds[pr])

    def consume(c, slot):
        off = pl.multiple_of(c * ck, ck)
        s = s_bufs[slot][...] + jnp.tile(bias_ref[pl.ds(off, ck), :], (1, 4))
        m = m_ref[...]
        m_new = jnp.maximum(m, jnp.max(s, axis=0, keepdims=True))
        alpha = jnp.exp2(m - m_new)
        p = jnp.exp2(s - m_new)
        l_ref[...] = alpha * l_ref[...] + jnp.sum(p, axis=0, keepdims=True)
        m_ref[...] = m_new
        p = p.astype(jnp.bfloat16)
        upd = jnp.concatenate(
            [_dot(vt_ref[h * HEAD_DIM:(h + 1) * HEAD_DIM, pl.ds(off, ck)], p[:, h * tq:(h + 1) * tq])
             for h in range(4)], axis=1)
        acc_ref[...] = alpha * acc_ref[...] + upd

    def pair_step(i, _):
        c = 2 * i
        logits(c + 1, 1)
        consume(c, 0)
        logits(jnp.minimum(c + 2, nblk - 1), 0)
        consume(c + 1, 1)
        return 0

    logits(0, 0)
    lax.fori_loop(0, nblk // 2, pair_step, 0)

    @pl.when(nblk % 2 == 1)
    def _():
        consume(nblk - 1, 0)

    out = acc_ref[...] / l_ref[...]
    out_t = jnp.concatenate([out[:, h * tq:(h + 1) * tq] for h in range(4)], axis=0)
    o_ref[0] = out_t.T.astype(o_ref.dtype)


def _dsa_attention(p3, small3, *, tq=256, ck=256):
    b, s, _ = p3.shape
    ck = min(ck, s)
    topk = min(TOPK_MAX, s // 4)
    return pl.pallas_call(
        functools.partial(_dsa_kernel, tq=tq, ck=ck, topk=topk, seq=s),
        grid=(b, s // tq),
        in_specs=[
            pl.BlockSpec((1, tq, 4 * LANES), lambda bi, qi: (bi, qi, BLK_IQ // 4)),
            pl.BlockSpec((1, tq, 2 * LANES), lambda bi, qi: (bi, qi, BLK_DQ // 2)),
            pl.BlockSpec((1, tq, LANES), lambda bi, qi: (bi, qi, 0)),
            pl.BlockSpec((1, s, LANES), lambda bi, qi: (bi, 0, BLK_IK)),
            pl.BlockSpec((1, s, 2 * LANES), lambda bi, qi: (bi, 0, BLK_DK // 2)),
            pl.BlockSpec((1, s, 2 * LANES), lambda bi, qi: (bi, 0, BLK_DV // 2)),
        ],
        out_specs=pl.BlockSpec((1, tq, 2 * LANES), lambda bi, qi: (bi, qi, 0)),
        out_shape=jax.ShapeDtypeStruct((b, s, 2 * LANES), jnp.bfloat16),
        scratch_shapes=[
            pltpu.VMEM((s, tq), jnp.int32),
            pltpu.VMEM((s, tq), jnp.float32),
            pltpu.VMEM((4 * HEAD_DIM, s), jnp.bfloat16),
            pltpu.VMEM((1, tq), jnp.int32),
            pltpu.VMEM((HEAD_DIM, 4 * tq), jnp.float32),
            pltpu.VMEM((1, 4 * tq), jnp.float32), pltpu.VMEM((1, 4 * tq), jnp.float32),
            pltpu.VMEM((ck, 4 * tq), jnp.float32), pltpu.VMEM((ck, 4 * tq), jnp.float32),
        ],
        compiler_params=_cparams(("arbitrary", "arbitrary")),
        name="dsa_attn",
    )(p3, p3, small3, p3, p3, p3)


def _merge_kernel(x_ref, g_ref, wgate_ref, bgate_ref, ya_ref, yb_ref, yc_ref, yd_ref,
                  wa_ref, wb_ref, wc_ref, wd_ref, wo_ref, o_ref):
    x = x_ref[...]
    d = x.shape[1]
    hb = _rms(x, g_ref[...], NORM_EPS).astype(jnp.bfloat16)
    merged = jnp.zeros(x.shape, jnp.float32)
    for i, (y_ref, w_ref) in enumerate(((ya_ref, wa_ref), (yb_ref, wb_ref), (yc_ref, wc_ref), (yd_ref, wd_ref))):
        gate = jax.nn.sigmoid(_dot(hb, wgate_ref[:, i * d:(i + 1) * d]) + bgate_ref[:, i * d:(i + 1) * d])
        merged = merged + gate * _dot(y_ref[...], w_ref[...])
    o_ref[...] = x + _dot(merged.astype(jnp.bfloat16), wo_ref[...])


def _merge(x2, gain, wgate, bgate, ya, yb, yc, yd, wa, wb, wc, wd, wo, *, tm=512):
    m, d = x2.shape
    rowblk = lambda w: pl.BlockSpec((tm, w), lambda i: (i, 0))
    return pl.pallas_call(
        _merge_kernel,
        grid=(m // tm,),
        in_specs=[
            rowblk(d), _resident((1, d)), _resident((d, N_BRANCH * d)), _resident((1, N_BRANCH * d)),
            rowblk(ya.shape[1]), rowblk(yb.shape[1]), rowblk(yc.shape[1]), rowblk(yd.shape[1]),
            _resident(wa.shape), _resident(wb.shape), _resident(wc.shape), _resident(wd.shape),
            _resident(wo.shape),
        ],
        out_specs=rowblk(d),
        out_shape=jax.ShapeDtypeStruct((m, d), jnp.float32),
        compiler_params=_cparams(("parallel",)),
        name="merge",
    )(x2, gain, wgate, bgate, ya, yb, yc, yd, wa, wb, wc, wd, wo)


def _proj_weights(w_in, b_fgt):
    d = w_in.shape[0]
    o = 0

    def take(width):
        nonlocal o
        seg = w_in[:, o:o + width]
        o += width
        return seg

    aq, ak, av = take(512), take(512), take(512)
    bq, bk, bv, bf = take(256), take(256), take(256), take(4)
    cq, ck, cv = take(256), take(256), take(256)
    dq, dk, dv = take(256), take(256), take(256)
    diq, dik, diw = take(512), take(64), take(8)
    zeros = jnp.zeros((d, LANES), w_in.dtype)
    big = jnp.concatenate([aq, ak, av, bq, bk, bv, cq, ck, cv, dq, dk, dv, dik, dik, zeros, diq], axis=1)
    small = jnp.concatenate([diw, bf, jnp.zeros((d, LANES - 12), w_in.dtype)], axis=1)
    bias = jnp.concatenate([jnp.zeros((8,), jnp.float32), b_fgt.astype(jnp.float32),
                            jnp.zeros((LANES - 12,), jnp.float32)]).reshape(1, LANES)
    return big.astype(jnp.bfloat16), small.astype(jnp.bfloat16), bias


def kernel(x, positions, ffn1_norm, ffn1_w_gu, ffn1_w_down, mix_norm, w_in, b_fgt, lam_q1, lam_k1, lam_q2, lam_k2, diff_gain, w_gate, b_gate, w_br_a, w_br_b, w_br_c, w_br_d, w_out, ffn2_norm, ffn2_w_gu, ffn2_w_down, final_norm):
    b, s, d = x.shape
    depth = w_in.shape[0]
    bf = jnp.bfloat16
    cos_t, sin1_t, sin2_t = _rope_tables(positions)
    fgain = final_norm.reshape(1, d)
    x2 = x.reshape(b * s, d)
    for l in range(depth):
        x2 = _ffn(x2, ffn1_norm[l].reshape(1, d), ffn1_w_gu[l][:, :D_FF].astype(bf),
                  ffn1_w_gu[l][:, D_FF:].astype(bf), ffn1_w_down[l].astype(bf), fgain, final_norm=False)
        wbig, wsmall, bsmall = _proj_weights(w_in[l], b_fgt[l])
        p2, small2 = _proj(x2, mix_norm[l].reshape(1, d), wbig, wsmall, bsmall, cos_t, sin1_t, sin2_t, s)
        p3 = p2.reshape(b, s, PROJ_W)
        small3 = small2.reshape(b, s, LANES)
        cum_n, cum_t = _cumsum(small3)
        lam_rows = jnp.zeros((8, LANES), jnp.float32)
        lam_rows = lam_rows.at[0:4, 0:HEAD_DIM].set(jnp.stack([lam_q1[l], lam_k1[l], lam_q2[l], lam_k2[l]]))
        ya = _diff_attention(p3, lam_rows, diff_gain[l].reshape(1, LANES), l)
        yb = _fox_attention(p3, cum_n, cum_t)
        yc = _sb_attention(p3)
        yd = _dsa_attention(p3, small3)
        x2 = _merge(x2, mix_norm[l].reshape(1, d), w_gate[l].astype(bf), b_gate[l].reshape(1, N_BRANCH * d),
                    ya.reshape(b * s, -1), yb.reshape(b * s, -1), yc.reshape(b * s, -1), yd.reshape(b * s, -1),
                    w_br_a[l].astype(bf), w_br_b[l].astype(bf), w_br_c[l].astype(bf), w_br_d[l].astype(bf),
                    w_out[l].astype(bf))
        x2 = _ffn(x2, ffn2_norm[l].reshape(1, d), ffn2_w_gu[l][:, :D_FF].astype(bf),
                  ffn2_w_gu[l][:, D_FF:].astype(bf), ffn2_w_down[l].astype(bf), fgain,
                  final_norm=(l == depth - 1))
    return x2.reshape(b, s, d)
```
